```python
import jax
import jax.numpy as jnp
from jax import lax
import numpy as np

D_MODEL = 2048
BATCH = 2
SEQ = 4096
DEPTH = 2
DEC_BATCH = 128
DEC_SEQ = 1
PAST_LEN = 2048
PAGE_SIZE = 128

HEAD_DIM = 128
ATT_HEADS = 12
ATT_W = ATT_HEADS * HEAD_DIM
SC_DIM = D_MODEL - ATT_W
SC_WIDTH = 3
DILATIONS = (1, 4, 16)
BAND = 128
WIN_MAX = BAND * DILATIONS[-1]
ATT_SCALE = HEAD_DIM ** -0.5
DN_QK_HEADS = 16
DN_V_HEADS = 32
DN_K_DIM = 128
DN_V_DIM = 128
DN_QK_W = DN_QK_HEADS * DN_K_DIM
DN_VW = DN_V_HEADS * DN_V_DIM
DN_CONV_DIM = 2 * DN_QK_W + DN_VW
DN_CONV_WIDTH = 4
DN_CHUNK = 64
N_EXPERTS = 64
N_GROUPS = 8
TOPK_GROUPS = 4
TOP_K = 8
EXPERT_DIM = 512
ROUTED_SCALE = 2.5
MOE_BLOCK = 128
PLE_DIM = 256
ALPHA = (2 * DEPTH) ** 0.25
BETA_INIT = (8 * DEPTH) ** -0.25
N_MIX_LAYERS = (DEPTH + 1) // 2
N_DELTA_LAYERS = DEPTH // 2
LN_EPS = 1e-5
RMS_EPS = 1e-6
L2_EPS = 1e-6
F32 = jnp.float32

kernel_name = 'hybrid_dilated_conv_deltanet_moe_step'


def _layer_norm(x, g, b):
    xf = x.astype(F32)
    mu = xf.mean(-1, keepdims=True)
    var = jnp.square(xf - mu).mean(-1, keepdims=True)
    return ((xf - mu) * lax.rsqrt(var + LN_EPS) * g + b).astype(x.dtype)


def _causal_dwconv(x, prefix, w):
    width, t = w.shape[0], x.shape[1]
    xp = jnp.concatenate([prefix.astype(x.dtype), x], axis=1)
    y = xp[:, 0:t] * w[0]
    for i in range(1, width):
        y = y + xp[:, i:i + t] * w[i]
    return y, xp[:, t:]


def _alibi_slopes():
    return jnp.exp2(-8.0 * (jnp.arange(ATT_HEADS, dtype=F32) + 1.0) / ATT_HEADS)


def _dilated_band(q, k, v, dil, slopes):
    bt, s, h, e = q.shape
    span = BAND * dil
    s_p = -(-s // span) * span
    nb = s_p // span

    def to_res(t):
        t = jnp.pad(t, ((0, 0), (0, s_p - s), (0, 0), (0, 0)))
        t = t.reshape(bt, s_p // dil, dil, h, e).transpose(0, 2, 1, 3, 4)
        return t.reshape(bt, dil, nb, BAND, h, e)

    def with_prev(t):
        prev = jnp.pad(t, ((0, 0), (0, 0), (1, 0), (0, 0), (0, 0), (0, 0)))[:, :, :nb]
        return jnp.concatenate([prev, t], axis=3)

    def from_res(t):
        t = t.reshape((bt, dil, nb * BAND) + t.shape[4:])
        t = jnp.swapaxes(t, 1, 2).reshape((bt, s_p) + t.shape[3:])
        return t[:, :s]

    qr = to_res(q)
    kb, vb = with_prev(to_res(k)), with_prev(to_res(v))
    dist = jnp.arange(BAND)[:, None] + BAND - jnp.arange(2 * BAND)[None, :]
    band_ok = (dist >= 0) & (dist <= BAND)
    first_ok = (jnp.arange(nb)[:, None, None] > 0) | (jnp.arange(2 * BAND)[None, None, :] >= BAND)
    mask = band_ok[None] & first_ok
    sc = jnp.einsum('brnqhe,brnkhe->brnhqk', qr, kb) * ATT_SCALE
    sc = sc - slopes[:, None, None] * (dist * dil).astype(F32)
    sc = jnp.where(mask[:, None], sc, -jnp.inf)
    m = sc.max(-1)
    p = jnp.exp(sc - m[..., None])
    l = p.sum(-1)
    acc = jnp.einsum('brnhqk,brnkhe->brnqhe', p, vb)
    return (from_res(m.transpose(0, 1, 2, 4, 3)), from_res(l.transpose(0, 1, 2, 4, 3)), from_res(acc))


def _dilated_gather(q, k_new, v_new, k_buf, v_buf, dil, slopes):
    n_buf, t_new = k_buf.shape[1], q.shape[1]
    steps = jnp.arange(BAND + 1)
    idx = n_buf + jnp.arange(t_new)[:, None] - steps[None, :] * dil
    valid = idx >= 0
    in_buf = (idx < n_buf)[None, :, :, None, None]
    i_buf = jnp.clip(idx, 0, n_buf - 1)
    i_new = jnp.clip(idx - n_buf, 0, t_new - 1)
    kg = jnp.where(in_buf, k_buf[:, i_buf].astype(F32), k_new[:, i_new])
    vg = jnp.where(in_buf, v_buf[:, i_buf].astype(F32), v_new[:, i_new])
    sc = jnp.einsum('bthe,btmhe->bthm', q, kg) * ATT_SCALE
    sc = sc - slopes[:, None] * (steps * dil).astype(F32)
    sc = jnp.where(valid[None, :, None, :], sc, -jnp.inf)
    m = sc.max(-1)
    p = jnp.exp(sc - m[..., None])
    return m, p.sum(-1), jnp.einsum('bthm,btmhe->bthe', p, vg)


def _merge_by_denominator(parts, dtype):
    m_max = jnp.stack([m for m, _, _ in parts]).max(0)
    m0, l0, a0 = parts[0]
    w0 = jnp.exp(m0 - m_max)
    num, den = w0[..., None] * a0, w0 * l0
    for m, l, acc in parts[1:]:
        wg = jnp.exp(m - m_max)
        num = num + wg[..., None] * acc
        den = den + wg * l
    return (num / den[..., None]).astype(dtype)


def _mixer_ab(x, w_in, sconv_w, w_out, k_buf, v_buf, sconv_prev):
    bt, t, _ = x.shape
    proj = x @ w_in
    q, k, v, h, gate_b, gate_c = jnp.split(
        proj, [ATT_W, 2 * ATT_W, 3 * ATT_W, 3 * ATT_W + SC_DIM, 3 * ATT_W + 2 * SC_DIM], axis=-1)

    def heads(z):
        return z.reshape(bt, t, ATT_HEADS, HEAD_DIM).astype(F32)

    q, k, v = heads(q), heads(k), heads(v)
    slopes = _alibi_slopes()
    if k_buf is None:
        parts = [_dilated_band(q, k, v, d, slopes) for d in DILATIONS]
        keep = min(WIN_MAX, t)
        k_rows, v_rows = k[:, t - keep:], v[:, t - keep:]
        prefix = jnp.zeros((bt, SC_WIDTH - 1, SC_DIM), x.dtype)
    else:
        parts = [_dilated_gather(q, k, v, k_buf, v_buf, d, slopes) for d in DILATIONS]
        k_rows, v_rows = k, v
        prefix = sconv_prev
    att = _merge_by_denominator(parts, x.dtype).reshape(bt, t, ATT_W)
    conv, sconv_new = _causal_dwconv(gate_c * h, prefix, sconv_w)
    y = jnp.concatenate([att, gate_b * conv], axis=-1) @ w_out
    return y, k_rows.astype(x.dtype), v_rows.astype(x.dtype), sconv_new


def _chunk_gated_delta(q, k, v, g, beta, s0):
    bt, t, h, _ = q.shape
    dv = v.shape[-1]
    t_p = -(-t // DN_CHUNK) * DN_CHUNK
    n = t_p // DN_CHUNK

    def chunks(z):
        z = jnp.pad(z, [(0, 0), (0, t_p - t)] + [(0, 0)] * (z.ndim - 2))
        z = z.reshape((bt, n, DN_CHUNK) + z.shape[2:])
        return jnp.moveaxis(z, (1, 3), (0, 2))

    qc, kc, vc, gc, bc = chunks(q), chunks(k), chunks(v), chunks(g), chunks(beta)
    gc = jnp.cumsum(gc, axis=-1)
    ar = jnp.arange(DN_CHUNK)
    tril = ar[:, None] >= ar[None, :]
    decay = jnp.exp(jnp.where(tril, gc[..., :, None] - gc[..., None, :], -jnp.inf))
    kbeta = kc * bc[..., None]
    a_mat = jnp.where(ar[:, None] > ar[None, :], jnp.einsum('nbhck,nbhdk->nbhcd', kbeta, kc) * decay, 0.0)
    rhs = jnp.concatenate([vc * bc[..., None], kbeta * jnp.exp(gc)[..., None]], axis=-1)
    sol = lax.linalg.triangular_solve(a_mat + jnp.eye(DN_CHUNK, dtype=F32), rhs, left_side=True, lower=True)
    u, w = sol[..., :dv], sol[..., dv:]

    def step(s, inp):
        q_i, k_i, u_i, w_i, g_i, d_i = inp
        v_new = u_i - jnp.einsum('bhck,bhkv->bhcv', w_i, s)
        attn = jnp.einsum('bhck,bhdk->bhcd', q_i, k_i) * d_i
        o = (jnp.einsum('bhck,bhkv->bhcv', q_i * jnp.exp(g_i)[..., None], s)
             + jnp.einsum('bhcd,bhdv->bhcv', attn, v_new))
        g_last = g_i[..., -1]
        s = (s * jnp.exp(g_last)[..., None, None]
             + jnp.einsum('bhck,bhcv->bhkv', k_i * jnp.exp(g_last[..., None] - g_i)[..., None], v_new))
        return s, o

    s_fin, o = lax.scan(step, s0, (qc, kc, u, w, gc, decay))
    o = jnp.moveaxis(o, (0, 2), (1, 3)).reshape(bt, t_p, h, dv)[:, :t]
    return o, s_fin


def _gated_deltanet(x, w_in, conv_w, a_log, dt_bias, norm_w, w_out, conv_prev, s0):
    bt, t, _ = x.shape
    proj = x @ w_in
    qkv, z, b_raw, a_raw = jnp.split(
        proj, [DN_CONV_DIM, DN_CONV_DIM + DN_VW, DN_CONV_DIM + DN_VW + DN_V_HEADS], axis=-1)
    if conv_prev is None:
        conv_prev = jnp.zeros((bt, DN_CONV_WIDTH - 1, DN_CONV_DIM), x.dtype)
    qkv, conv_new = _causal_dwconv(qkv, conv_prev, conv_w)
    qkv = jax.nn.silu(qkv).astype(F32)
    q, k, v = jnp.split(qkv, [DN_QK_W, 2 * DN_QK_W], axis=-1)

    def qk_heads(y):
        y = y.reshape(bt, t, DN_QK_HEADS, DN_K_DIM)
        y = y * lax.rsqrt(jnp.sum(y * y, -1, keepdims=True) + L2_EPS)
        return jnp.repeat(y, DN_V_HEADS // DN_QK_HEADS, axis=2)

    q = qk_heads(q) * (DN_K_DIM ** -0.5)
    k = qk_heads(k)
    v = v.reshape(bt, t, DN_V_HEADS, DN_V_DIM)
    beta = jax.nn.sigmoid(b_raw.astype(F32))
    g = -jnp.exp(a_log.astype(F32)) * jax.nn.softplus(a_raw.astype(F32) + dt_bias.astype(F32))
    if s0 is None:
        s0 = jnp.zeros((bt, DN_V_HEADS, DN_K_DIM, DN_V_DIM), F32)
    o, s_new = _chunk_gated_delta(q, k, v, g, beta, s0.astype(F32))
    zf = z.reshape(bt, t, DN_V_HEADS, DN_V_DIM).astype(F32)
    o = o * lax.rsqrt(jnp.mean(o * o, -1, keepdims=True) + RMS_EPS) * norm_w * jax.nn.silu(zf)
    y = o.reshape(bt, t, DN_VW).astype(x.dtype) @ w_out
    return y, conv_new, s_new


def _swiglu(x, wg, wu, wd):
    return (jax.nn.silu(x @ wg) * (x @ wu)) @ wd


def _route(x2, w_router, e_bias):
    t = x2.shape[0]
    scores = jax.nn.sigmoid((x2 @ w_router).astype(F32))
    sel = scores + e_bias.astype(F32)
    per_group = sel.reshape(t, N_GROUPS, N_EXPERTS // N_GROUPS)
    group_score = lax.top_k(per_group, 2)[0].sum(-1)
    _, g_idx = lax.top_k(group_score, TOPK_GROUPS)
    g_mask = jax.nn.one_hot(g_idx, N_GROUPS, dtype=F32).sum(1) > 0
    e_mask = jnp.repeat(g_mask, N_EXPERTS // N_GROUPS, axis=1)
    _, e_idx = lax.top_k(jnp.where(e_mask, sel, -jnp.inf), TOP_K)
    wts = jnp.take_along_axis(scores, e_idx, axis=1)
    wts = wts / wts.sum(-1, keepdims=True) * ROUTED_SCALE
    return e_idx, wts


def _routed_experts(x2, e_idx, wts, w_gate, w_up, w_down):
    t, d = x2.shape
    n_assign = t * TOP_K
    flat_e = e_idx.reshape(-1)
    order = jnp.argsort(flat_e)
    sorted_e = flat_e[order]
    sorted_tok = order // TOP_K
    counts = jnp.bincount(flat_e, length=N_EXPERTS)
    padded = (counts + MOE_BLOCK - 1) // MOE_BLOCK * MOE_BLOCK
    pad_end = jnp.cumsum(padded)
    pad_start = pad_end - padded
    start = jnp.cumsum(counts) - counts
    dest = pad_start[sorted_e] + jnp.arange(n_assign) - start[sorted_e]
    n_blocks = -(-(n_assign + N_EXPERTS * (MOE_BLOCK - 1)) // MOE_BLOCK)
    n_rows = n_blocks * MOE_BLOCK
    row_tok = jnp.full((n_rows,), t, jnp.int32).at[dest].set(sorted_tok.astype(jnp.int32))
    block_e = jnp.minimum(jnp.searchsorted(pad_end, jnp.arange(n_blocks) * MOE_BLOCK, side='right'),
                          N_EXPERTS - 1)
    x_rows = jnp.concatenate([x2, jnp.zeros((1, d), x2.dtype)], axis=0)[row_tok]
    x_rows = x_rows.reshape(n_blocks, MOE_BLOCK, d)

    def expert_block(args):
        xb, e = args
        return _swiglu(xb, w_gate[e], w_up[e], w_down[e])

    y_rows = lax.map(expert_block, (x_rows, block_e)).reshape(n_rows, d)
    dest_of_assign = jnp.zeros((n_assign,), jnp.int32).at[order].set(dest.astype(jnp.int32))
    y_assign = y_rows[dest_of_assign].reshape(t, TOP_K, d)
    return jnp.einsum('tk,tkd->td', wts.astype(y_assign.dtype), y_assign)


def _moe(x2, w_router, e_bias, w_gate, w_up, w_down, ws_gate, ws_up, ws_down):
    e_idx, wts = _route(x2, w_router, e_bias)
    return _routed_experts(x2, e_idx, wts, w_gate, w_up, w_down) + _swiglu(x2, ws_gate, ws_up, ws_down)


def _trunk(x, p, st, w):
    bt, t, _ = x.shape
    new_k, new_v, new_sc, new_dc, new_ds = [], [], [], [], []
    for i in range(DEPTH):
        j = i // 2
        if i % 2 == 0:
            mix, k_rows, v_rows, sc = _mixer_ab(
                x, w['mix_w_in'][j], w['mix_sconv_w'][j], w['mix_w_out'][j],
                None if st is None else st['k'][j],
                None if st is None else st['v'][j],
                None if st is None else st['sconv'][j])
            new_k.append(k_rows)
            new_v.append(v_rows)
            new_sc.append(sc)
        else:
            mix, dc, ds = _gated_deltanet(
                x, w['dn_w_in'][j], w['dn_conv_w'][j], w['dn_a_log'][j], w['dn_dt_bias'][j],
                w['dn_norm_w'][j], w['dn_w_out'][j],
                None if st is None else st['dconv'][j],
                None if st is None else st['delta'][j])
            new_dc.append(dc)
            new_ds.append(ds)
        x = _layer_norm(ALPHA * x + mix, w['ln1_g'][i], w['ln1_b'][i])
        h = _moe(x.reshape(bt * t, D_MODEL), w['moe_w_router'][i], w['moe_e_bias'][i],
                 w['moe_w_gate'][i], w['moe_w_up'][i], w['moe_w_down'][i],
                 w['moe_ws_gate'][i], w['moe_ws_up'][i], w['moe_ws_down'][i]).reshape(bt, t, D_MODEL)
        ple = jax.nn.sigmoid(x @ w['ple_w_gate'][i]) * (p[i] @ w['ple_w_proj'][i])
        x = _layer_norm(ALPHA * x + h + ple, w['ln2_g'][i], w['ln2_b'][i])
    return x, jnp.stack(new_k), jnp.stack(new_v), jnp.stack(new_sc), jnp.stack(new_dc), jnp.stack(new_ds)


def setup_inputs(seed: int = 0) -> dict:
    key = jax.random.key(seed)
    ks = iter(jax.random.split(key, 48))

    def nrm(shape, scale):
        return jax.random.normal(next(ks), shape, F32) * scale

    n_buf = min(WIN_MAX, PAST_LEN)
    d = D_MODEL
    mix_cols = jnp.concatenate([jnp.ones((2 * ATT_W,), F32), jnp.full((ATT_W + SC_DIM,), BETA_INIT, F32),
                                jnp.ones((2 * SC_DIM,), F32)])
    dn_cols = jnp.concatenate([jnp.ones((2 * DN_QK_W,), F32), jnp.full((DN_VW,), BETA_INIT, F32),
                               jnp.ones((DN_VW + 2 * DN_V_HEADS,), F32)])
    return {
        'x_prompt': nrm((BATCH, SEQ, d), 1.0),
        'x_sample': nrm((DEC_BATCH, DEC_SEQ, d), 1.0),
        'cache_attn_k': nrm((N_MIX_LAYERS, DEC_BATCH, n_buf, ATT_HEADS, HEAD_DIM), 1.0),
        'cache_attn_v': nrm((N_MIX_LAYERS, DEC_BATCH, n_buf, ATT_HEADS, HEAD_DIM), 1.0),
        'state_sconv': nrm((N_MIX_LAYERS, DEC_BATCH, SC_WIDTH - 1, SC_DIM), 1.0),
        'state_dconv': nrm((N_DELTA_LAYERS, DEC_BATCH, DN_CONV_WIDTH - 1, DN_CONV_DIM), 1.0),
        'state_delta': nrm((N_DELTA_LAYERS, DEC_BATCH, DN_V_HEADS, DN_K_DIM, DN_V_DIM), 0.1),
        'p_prompt': nrm((DEPTH, BATCH, SEQ, PLE_DIM), 1.0),
        'p_sample': nrm((DEPTH, DEC_BATCH, DEC_SEQ, PLE_DIM), 1.0),
        'mix_w_in': nrm((N_MIX_LAYERS, d, 3 * ATT_W + 3 * SC_DIM), d ** -0.5) * mix_cols,
        'mix_sconv_w': nrm((N_MIX_LAYERS, SC_WIDTH, SC_DIM), SC_WIDTH ** -0.5),
        'mix_w_out': nrm((N_MIX_LAYERS, ATT_W + SC_DIM, d), (ATT_W + SC_DIM) ** -0.5 * BETA_INIT),
        'dn_w_in': nrm((N_DELTA_LAYERS, d, DN_CONV_DIM + DN_VW + 2 * DN_V_HEADS), d ** -0.5) * dn_cols,
        'dn_conv_w': nrm((N_DELTA_LAYERS, DN_CONV_WIDTH, DN_CONV_DIM), DN_CONV_WIDTH ** -0.5),
        'dn_a_log': jnp.log(jax.random.uniform(next(ks), (N_DELTA_LAYERS, DN_V_HEADS), F32, 1.0, 16.0)),
        'dn_dt_bias': nrm((N_DELTA_LAYERS, DN_V_HEADS), 0.1),
        'dn_norm_w': 1.0 + nrm((N_DELTA_LAYERS, DN_V_DIM), 0.01),
        'dn_w_out': nrm((N_DELTA_LAYERS, DN_VW, d), DN_VW ** -0.5 * BETA_INIT),
        'ln1_g': 1.0 + nrm((DEPTH, d), 0.01),
        'ln1_b': nrm((DEPTH, d), 0.01),
        'ln2_g': 1.0 + nrm((DEPTH, d), 0.01),
        'ln2_b': nrm((DEPTH, d), 0.01),
        'moe_w_router': nrm((DEPTH, d, N_EXPERTS), d ** -0.5),
        'moe_e_bias': nrm((DEPTH, N_EXPERTS), 0.01),
        'moe_w_gate': nrm((DEPTH, N_EXPERTS, d, EXPERT_DIM), d ** -0.5),
        'moe_w_up': nrm((DEPTH, N_EXPERTS, d, EXPERT_DIM), d ** -0.5 * BETA_INIT),
        'moe_w_down': nrm((DEPTH, N_EXPERTS, EXPERT_DIM, d), EXPERT_DIM ** -0.5 * BETA_INIT),
        'moe_ws_gate': nrm((DEPTH, d, EXPERT_DIM), d ** -0.5),
        'moe_ws_up': nrm((DEPTH, d, EXPERT_DIM), d ** -0.5 * BETA_INIT),
        'moe_ws_down': nrm((DEPTH, EXPERT_DIM, d), EXPERT_DIM ** -0.5 * BETA_INIT),
        'ple_w_proj': nrm((DEPTH, PLE_DIM, d), PLE_DIM ** -0.5),
        'ple_w_gate': nrm((DEPTH, d, d), d ** -0.5),
    }


def reference(x_prompt, x_sample, cache_attn_k, cache_attn_v, state_sconv, state_dconv, state_delta,
              p_prompt, p_sample, mix_w_in, mix_sconv_w, mix_w_out, dn_w_in, dn_conv_w, dn_a_log,
              dn_dt_bias, dn_norm_w, dn_w_out, ln1_g, ln1_b, ln2_g, ln2_b, moe_w_router, moe_e_bias,
              moe_w_gate, moe_w_up, moe_w_down, moe_ws_gate, moe_ws_up, moe_ws_down, ple_w_proj,
              ple_w_gate):
    w = {
        'mix_w_in': mix_w_in, 'mix_sconv_w': mix_sconv_w, 'mix_w_out': mix_w_out,
        'dn_w_in': dn_w_in, 'dn_conv_w': dn_conv_w, 'dn_a_log': dn_a_log, 'dn_dt_bias': dn_dt_bias,
        'dn_norm_w': dn_norm_w, 'dn_w_out': dn_w_out,
        'ln1_g': ln1_g, 'ln1_b': ln1_b, 'ln2_g': ln2_g, 'ln2_b': ln2_b,
        'moe_w_router': moe_w_router, 'moe_e_bias': moe_e_bias, 'moe_w_gate': moe_w_gate,
        'moe_w_up': moe_w_up, 'moe_w_down': moe_w_down, 'moe_ws_gate': moe_ws_gate,
        'moe_ws_up': moe_ws_up, 'moe_ws_down': moe_ws_down,
        'ple_w_proj': ple_w_proj, 'ple_w_gate': ple_w_gate,
    }
    y_prompt, k_p, v_p, sc_p, dc_p, ds_p = _trunk(x_prompt, p_prompt, None, w)
    st = {'k': cache_attn_k, 'v': cache_attn_v, 'sconv': state_sconv, 'dconv': state_dconv,
          'delta': state_delta}
    y_sample, k_s, v_s, sc_s, dc_s, ds_s = _trunk(x_sample, p_sample, st, w)
    return (y_prompt, y_sample, k_p, v_p, sc_p, dc_p, ds_p, k_s, v_s, sc_s, dc_s, ds_s)
```

```python
import functools

import jax
import jax.numpy as jnp
from jax import lax
from jax.experimental import pallas as pl
from jax.experimental.pallas import tpu as pltpu

F32 = jnp.float32
BF16 = jnp.bfloat16

DEPTH = 2
HEAD_DIM = 128
ATT_HEADS = 12
ATT_W = ATT_HEADS * HEAD_DIM
SC_WIDTH = 3
DILATIONS = (1, 4, 16)
BAND = 128
SLAB = BAND * DILATIONS[-1]
ATT_SCALE = HEAD_DIM ** -0.5
DN_QK_HEADS = 16
DN_V_HEADS = 32
DN_DIM = 128
DN_QK_W = DN_QK_HEADS * DN_DIM
DN_VW = DN_V_HEADS * DN_DIM
DN_CONV_DIM = 2 * DN_QK_W + DN_VW
DN_CONV_WIDTH = 4
DN_CHUNK = 64
DN_GROUP_QK = 4
DN_GROUP_V = 2 * DN_GROUP_QK
N_EXPERTS = 64
N_GROUPS = 8
TOPK_GROUPS = 4
TOP_K = 8
ROUTED_SCALE = 2.5
ALPHA = (2 * DEPTH) ** 0.25
LN_EPS = 1e-5
RMS_EPS = 1e-6
L2_EPS = 1e-6
NEG = -1e30

MOE_TM = 256
VMEM_LIMIT = 56 * 1024 * 1024


def _sds(shape, dtype):
    return jax.ShapeDtypeStruct(shape, dtype)


def _cparams(sem):
    return pltpu.CompilerParams(dimension_semantics=sem, vmem_limit_bytes=VMEM_LIMIT)


def _silu(x):
    return x * jax.nn.sigmoid(x)


def _layer_norm_rows(v, g, b):
    mu = jnp.mean(v, axis=-1, keepdims=True)
    c = v - mu
    var = jnp.mean(c * c, axis=-1, keepdims=True)
    return c * lax.rsqrt(var + LN_EPS) * g + b


def _split3(x):
    hi = x.astype(BF16)
    r1 = x - hi.astype(F32)
    mid = r1.astype(BF16)
    lo = (r1 - mid.astype(F32)).astype(BF16)
    return hi, mid, lo


def _dot(a, b):
    return jnp.dot(a, b, preferred_element_type=F32)


def _dot_nt(a, b):
    return lax.dot_general(a, b, (((1,), (1,)), ((), ())), preferred_element_type=F32)


def _dot_tn(a, b):
    return lax.dot_general(a, b, (((0,), (0,)), ((), ())), preferred_element_type=F32)


def _dot_exact_lhs(sel_bf16, x):
    hi, mid, lo = _split3(x)
    return _dot(sel_bf16, hi) + _dot(sel_bf16, mid) + _dot(sel_bf16, lo)


def _dot_exact_rhs(x, sel_bf16):
    hi, mid, lo = _split3(x)
    return _dot(hi, sel_bf16) + _dot(mid, sel_bf16) + _dot(lo, sel_bf16)


def _dot3(a, b):
    a_hi = a.astype(BF16)
    a_lo = (a - a_hi.astype(F32)).astype(BF16)
    b_hi = b.astype(BF16)
    b_lo = (b - b_hi.astype(F32)).astype(BF16)
    return _dot(a_hi, b_hi) + _dot(a_hi, b_lo) + _dot(a_lo, b_hi)


def _mm_kernel(x_ref, w_ref, o_ref):
    o_ref[...] = _dot(x_ref[...], w_ref[...])


def _matmul(x, w, tm, tn):
    m, k = x.shape
    n = w.shape[1]
    assert m % tm == 0 and n % tn == 0
    return pl.pallas_call(
        _mm_kernel,
        out_shape=_sds((m, n), F32),
        grid=(m // tm, n // tn),
        in_specs=[pl.BlockSpec((tm, k), lambda i, j: (i, 0)),
                  pl.BlockSpec((k, tn), lambda i, j: (0, j))],
        out_specs=pl.BlockSpec((tm, tn), lambda i, j: (i, j)),
        compiler_params=_cparams(("parallel", "parallel")),
        name="matmul",
    )(x, w)


def _mm_ln_kernel(a_ref, w_ref, x_ref, g_ref, b_ref, of_ref, ob_ref, acc_ref, *, nk):
    kk = pl.program_id(1)
    part = _dot(a_ref[...], w_ref[...])

    @pl.when(kk == 0)
    def _():
        acc_ref[...] = part

    @pl.when(kk > 0)
    def _():
        acc_ref[...] += part

    @pl.when(kk == nk - 1)
    def _():
        y = _layer_norm_rows(ALPHA * x_ref[...] + acc_ref[...], g_ref[...], b_ref[...])
        of_ref[...] = y
        ob_ref[...] = y.astype(BF16)


def _matmul_residual_ln(a, w, x, g, b, tm, tk):
    m, k = a.shape
    d = w.shape[1]
    assert m % tm == 0 and k % tk == 0
    nk = k // tk
    return pl.pallas_call(
        functools.partial(_mm_ln_kernel, nk=nk),
        out_shape=(_sds((m, d), F32), _sds((m, d), BF16)),
        grid=(m // tm, nk),
        in_specs=[pl.BlockSpec((tm, tk), lambda i, kk: (i, kk)),
                  pl.BlockSpec((tk, d), lambda i, kk: (kk, 0)),
                  pl.BlockSpec((tm, d), lambda i, kk: (i, 0)),
                  pl.BlockSpec((1, d), lambda i, kk: (0, 0)),
                  pl.BlockSpec((1, d), lambda i, kk: (0, 0))],
        out_specs=(pl.BlockSpec((tm, d), lambda i, kk: (i, 0)),
                   pl.BlockSpec((tm, d), lambda i, kk: (i, 0))),
        scratch_shapes=[pltpu.VMEM((tm, d), F32)],
        compiler_params=_cparams(("parallel", "arbitrary")),
        name="matmul_residual_ln",
    )(a, w, x, g.reshape(1, d), b.reshape(1, d))


def _moe_kernel(te_ref, tv_ref, x_ref, wg_ref, wu_ref, wd_ref, o_ref, wg_s, wu_s, wd_s):
    i = pl.program_id(0)
    e = te_ref[i]
    prev = te_ref[jnp.maximum(i - 1, 0)]

    @pl.when(jnp.logical_or(i == 0, e != prev))
    def _():
        wg_s[...] = wg_ref[...].astype(BF16)
        wu_s[...] = wu_ref[...].astype(BF16)
        wd_s[...] = wd_ref[...].astype(BF16)

    @pl.when(tv_ref[i] > 0)
    def _():
        x = x_ref[...]
        a = _dot(x, wg_s[...])
        u = _dot(x, wu_s[...])
        o_ref[...] = _dot((_silu(a) * u).astype(BF16), wd_s[...]).astype(o_ref.dtype)

    @pl.when(tv_ref[i] == 0)
    def _():
        o_ref[...] = jnp.zeros_like(o_ref)


def _moe_experts(x_rows, tile_expert, tile_valid, w_gate, w_up, w_down):
    n_rows, d = x_rows.shape
    n_tiles = n_rows // MOE_TM
    de = w_gate.shape[-1]
    grid_spec = pltpu.PrefetchScalarGridSpec(
        num_scalar_prefetch=2,
        grid=(n_tiles,),
        in_specs=[pl.BlockSpec((MOE_TM, d), lambda i, te, tv: (i, 0)),
                  pl.BlockSpec((None, d, de), lambda i, te, tv: (te[i], 0, 0)),
                  pl.BlockSpec((None, d, de), lambda i, te, tv: (te[i], 0, 0)),
                  pl.BlockSpec((None, de, d), lambda i, te, tv: (te[i], 0, 0))],
        out_specs=pl.BlockSpec((MOE_TM, d), lambda i, te, tv: (i, 0)),
        scratch_shapes=[pltpu.VMEM((d, de), BF16), pltpu.VMEM((d, de), BF16),
                        pltpu.VMEM((de, d), BF16)],
    )
    return pl.pallas_call(
        _moe_kernel,
        out_shape=_sds((n_rows, d), BF16),
        grid_spec=grid_spec,
        compiler_params=_cparams(("arbitrary",)),
        name="moe_experts",
    )(tile_expert, tile_valid, x_rows, w_gate, w_up, w_down)


def _route(logits, e_bias):
    t = logits.shape[0]
    scores = jax.nn.sigmoid(logits)
    sel = scores + e_bias.astype(F32)
    per_group = sel.reshape(t, N_GROUPS, N_EXPERTS // N_GROUPS)
    group_score = lax.top_k(per_group, 2)[0].sum(-1)
    _, g_idx = lax.top_k(group_score, TOPK_GROUPS)
    g_mask = jax.nn.one_hot(g_idx, N_GROUPS, dtype=F32).sum(1) > 0
    e_mask = jnp.repeat(g_mask, N_EXPERTS // N_GROUPS, axis=1)
    _, e_idx = lax.top_k(jnp.where(e_mask, sel, -jnp.inf), TOP_K)
    wts = jnp.take_along_axis(scores, e_idx, axis=1)
    wts = wts / wts.sum(-1, keepdims=True) * ROUTED_SCALE
    return e_idx, wts


def _dispatch_plan(e_idx):
    t = e_idx.shape[0]
    n_assign = t * TOP_K
    flat_e = e_idx.reshape(-1)
    order = jnp.argsort(flat_e)
    sorted_e = flat_e[order]
    sorted_tok = (order // TOP_K).astype(jnp.int32)
    counts = jnp.bincount(flat_e, length=N_EXPERTS)
    padded = (counts + MOE_TM - 1) // MOE_TM * MOE_TM
    pad_end = jnp.cumsum(padded)
    pad_start = pad_end - padded
    start = jnp.cumsum(counts) - counts
    dest = (pad_start[sorted_e] + jnp.arange(n_assign) - start[sorted_e]).astype(jnp.int32)
    n_tiles = -(-(n_assign + N_EXPERTS * (MOE_TM - 1)) // MOE_TM)
    row_tok = jnp.zeros((n_tiles * MOE_TM,), jnp.int32).at[dest].set(sorted_tok)
    tile_start = jnp.arange(n_tiles) * MOE_TM
    tile_expert = jnp.minimum(jnp.searchsorted(pad_end, tile_start, side='right'),
                              N_EXPERTS - 1).astype(jnp.int32)
    tile_valid = (tile_start < pad_end[-1]).astype(jnp.int32)
    last_used = tile_expert[jnp.maximum(pad_end[-1] // MOE_TM - 1, 0)]
    tile_expert = jnp.where(tile_valid > 0, tile_expert, last_used)
    dest_of_assign = jnp.zeros((n_assign,), jnp.int32).at[order].set(dest)
    return row_tok, tile_expert, tile_valid, dest_of_assign


def _final_kernel(xf_ref, xb_ref, h_ref, p_ref, wsg_ref, wsu_ref, wsd_ref, wpg_ref, wpp_ref,
                  g_ref, b_ref, of_ref, ob_ref):
    x = xb_ref[...]
    a = _dot(x, wsg_ref[...])
    u = _dot(x, wsu_ref[...])
    shared = _dot((_silu(a) * u).astype(BF16), wsd_ref[...])
    gate = jax.nn.sigmoid(_dot(x, wpg_ref[...]))
    ple = gate * _dot(p_ref[...], wpp_ref[...])
    v = ALPHA * xf_ref[...] + (h_ref[...] + shared) + ple
    y = _layer_norm_rows(v, g_ref[...], b_ref[...])
    of_ref[...] = y
    ob_ref[...] = y.astype(BF16)


def _final_block(xf, xb, h, p, wsg, wsu, wsd, wpg, wpp, g, b, tm):
    m, d = xf.shape
    de = wsg.shape[1]
    dp = p.shape[1]
    assert m % tm == 0
    row = lambda i: (i, 0)
    const = lambda i: (0, 0)
    once = dict(pipeline_mode=pl.Buffered(1))
    return pl.pallas_call(
        _final_kernel,
        out_shape=(_sds((m, d), F32), _sds((m, d), BF16)),
        grid=(m // tm,),
        in_specs=[pl.BlockSpec((tm, d), row), pl.BlockSpec((tm, d), row),
                  pl.BlockSpec((tm, d), row), pl.BlockSpec((tm, dp), row),
                  pl.BlockSpec((d, de), const, **once), pl.BlockSpec((d, de), const, **once),
                  pl.BlockSpec((de, d), const, **once), pl.BlockSpec((d, d), const, **once),
                  pl.BlockSpec((dp, d), const, **once),
                  pl.BlockSpec((1, d), const), pl.BlockSpec((1, d), const)],
        out_specs=(pl.BlockSpec((tm, d), row), pl.BlockSpec((tm, d), row)),
        compiler_params=_cparams(("parallel",)),
        name="shared_ple_ln",
    )(xf, xb, h, p, wsg, wsu, wsd, wpg, wpp, g.reshape(1, d), b.reshape(1, d))


def _attn_prompt_kernel(slopes_ref, q_ref, k_ref, v_ref, o_ref, kk, vv, m_s, l_s, acc_s):
    h = pl.program_id(1)
    sl = pl.program_id(2)

    @pl.when(sl == 0)
    def _():
        kk[0:SLAB, :] = jnp.zeros((SLAB, HEAD_DIM), F32)
        vv[0:SLAB, :] = jnp.zeros((SLAB, HEAD_DIM), F32)

    kk[SLAB:2 * SLAB, :] = k_ref[...]
    vv[SLAB:2 * SLAB, :] = v_ref[...]

    slope = slopes_ref[h]
    ii = lax.broadcasted_iota(jnp.int32, (BAND, BAND), 0)
    jj = lax.broadcasted_iota(jnp.int32, (BAND, BAND), 1)
    dist_prev = (ii + BAND - jj).astype(F32)
    dist_cur = (ii - jj).astype(F32)

    for di, d in enumerate(DILATIONS):
        c = slope * float(d)
        bias_prev = jnp.where(jj >= ii, -c * dist_prev, NEG)
        bias_cur = jnp.where(jj <= ii, -c * dist_cur, NEG)

        def unit(u, carry, di=di, d=d, bias_prev=bias_prev, bias_cur=bias_cur):
            sp = u // d
            r = u - sp * d
            base = sp * (BAND * d) + r

            def rows(start):
                if d == 1:
                    return pl.ds(start, BAND)
                return pl.ds(start, BAND, stride=d)

            q = q_ref[rows(base), :].astype(BF16)
            kc = kk[rows(base + SLAB), :].astype(BF16)
            vc = vv[rows(base + SLAB), :].astype(BF16)
            kp = kk[rows(base + SLAB - BAND * d), :].astype(BF16)
            vp = vv[rows(base + SLAB - BAND * d), :].astype(BF16)
            pen = jnp.where(jnp.logical_and(sl == 0, sp == 0), NEG, 0.0)
            s_p = _dot_nt(q, kp) * ATT_SCALE + (bias_prev + pen)
            s_c = _dot_nt(q, kc) * ATT_SCALE + bias_cur
            m = jnp.maximum(jnp.max(s_p, axis=-1, keepdims=True), jnp.max(s_c, axis=-1, keepdims=True))
            p_p = jnp.exp(s_p - m)
            p_c = jnp.exp(s_c - m)
            l = jnp.sum(p_p, axis=-1, keepdims=True) + jnp.sum(p_c, axis=-1, keepdims=True)
            m_s[di, rows(base), :] = jnp.broadcast_to(m, (BAND, HEAD_DIM))
            l_s[di, rows(base), :] = jnp.broadcast_to(l, (BAND, HEAD_DIM))
            acc_s[di, rows(base), :] = _dot(p_p.astype(BF16), vp) + _dot(p_c.astype(BF16), vc)
            return carry

        lax.fori_loop(0, SLAB // BAND, unit, 0)

    def merge(t, carry):
        rows = pl.ds(pl.multiple_of(t * BAND, BAND), BAND)
        m_max = jnp.maximum(jnp.maximum(m_s[0, rows, :], m_s[1, rows, :]), m_s[2, rows, :])
        w0 = jnp.exp(m_s[0, rows, :] - m_max)
        num = w0 * acc_s[0, rows, :]
        den = w0 * l_s[0, rows, :]
        for di in range(1, len(DILATIONS)):
            w = jnp.exp(m_s[di, rows, :] - m_max)
            num = num + w * acc_s[di, rows, :]
            den = den + w * l_s[di, rows, :]
        o_ref[rows, :] = (num / den).astype(o_ref.dtype)
        return carry

    lax.fori_loop(0, SLAB // BAND, merge, 0)
    kk[0:SLAB, :] = kk[SLAB:2 * SLAB, :]
    vv[0:SLAB, :] = vv[SLAB:2 * SLAB, :]


def _attn_prompt(proj, slopes, n_batch, seq):
    assert seq % SLAB == 0
    ns = seq // SLAB
    grid_spec = pltpu.PrefetchScalarGridSpec(
        num_scalar_prefetch=1,
        grid=(n_batch, ATT_HEADS, ns),
        in_specs=[pl.BlockSpec((SLAB, HEAD_DIM), lambda b, h, s, sl: (b * ns + s, h)),
                  pl.BlockSpec((SLAB, HEAD_DIM), lambda b, h, s, sl: (b * ns + s, ATT_HEADS + h)),
                  pl.BlockSpec((SLAB, HEAD_DIM), lambda b, h, s, sl: (b * ns + s, 2 * ATT_HEADS + h))],
        out_specs=pl.BlockSpec((SLAB, HEAD_DIM), lambda b, h, s, sl: (b * ns + s, h)),
        scratch_shapes=[pltpu.VMEM((2 * SLAB, HEAD_DIM), F32), pltpu.VMEM((2 * SLAB, HEAD_DIM), F32),
                        pltpu.VMEM((len(DILATIONS), SLAB, HEAD_DIM), F32),
                        pltpu.VMEM((len(DILATIONS), SLAB, HEAD_DIM), F32),
                        pltpu.VMEM((len(DILATIONS), SLAB, HEAD_DIM), F32)],
    )
    return pl.pallas_call(
        _attn_prompt_kernel,
        out_shape=_sds((n_batch * seq, ATT_W), BF16),
        grid_spec=grid_spec,
        compiler_params=_cparams(("parallel", "parallel", "arbitrary")),
        name="attn_prompt",
    )(slopes, proj, proj, proj)


def _attn_sample_kernel(q_ref, kn_ref, vn_ref, k1_ref, k4_ref, k16_ref, v1_ref, v4_ref, v16_ref,
                        eh_ref, eht_ref, slopes_ref, o_ref):
    q = q_ref[...]
    eh = eh_ref[...]
    eht = eht_ref[...]
    slopes = slopes_ref[...]

    def rows8(x):
        return jnp.broadcast_to(x, (8, x.shape[-1]))

    def expand(x):
        return _dot_exact_rhs(rows8(x), eht)[0:1]

    s_self = _dot_exact_rhs(rows8(q * kn_ref[...]), eh)[0:1] * ATT_SCALE
    steps = (BAND - lax.broadcasted_iota(jnp.int32, (BAND, 1), 0)).astype(F32)
    parts = []
    for d, k_ref, v_ref in ((1, k1_ref, v1_ref), (4, k4_ref, v4_ref), (16, k16_ref, v16_ref)):
        sc = _dot_exact_rhs(k_ref[...] * q, eh) * ATT_SCALE - slopes * (steps * float(d))
        m_d = jnp.maximum(jnp.max(sc, axis=0, keepdims=True), s_self)
        p = jnp.exp(sc - m_d)
        p_self = jnp.exp(s_self - m_d)
        l_d = jnp.sum(p, axis=0, keepdims=True) + p_self
        pe = _dot_exact_rhs(p, eht)
        acc = jnp.sum(pe * v_ref[...], axis=0, keepdims=True) + expand(p_self) * vn_ref[...]
        parts.append((m_d, l_d, acc))
    m_max = jnp.maximum(jnp.maximum(parts[0][0], parts[1][0]), parts[2][0])
    num = jnp.zeros((1, ATT_W), F32)
    den = jnp.zeros((1, HEAD_DIM), F32)
    for m_d, l_d, acc in parts:
        w = jnp.exp(m_d - m_max)
        num = num + expand(w) * acc
        den = den + w * l_d
    o_ref[...] = (num * expand(1.0 / den)).astype(o_ref.dtype)


def _attn_sample(q, k_new, v_new, k_buf, v_buf, slopes):
    nb, n_buf, _ = k_buf.shape
    assert n_buf == BAND * DILATIONS[-1]
    views, specs = [], []
    for buf in (k_buf, v_buf):
        views += [buf.reshape(nb, n_buf // BAND, BAND, ATT_W),
                  buf.reshape(nb, n_buf // 4, 4 * ATT_W),
                  buf.reshape(nb, n_buf // 16, 16 * ATT_W)]
        specs += [pl.BlockSpec((None, None, BAND, ATT_W), lambda b: (b, n_buf // BAND - 1, 0, 0)),
                  pl.BlockSpec((None, BAND, ATT_W), lambda b: (b, n_buf // 4 // BAND - 1, 0)),
                  pl.BlockSpec((None, BAND, ATT_W), lambda b: (b, 0, 0))]
    head_of_col = jnp.arange(ATT_W) // HEAD_DIM
    eh = (head_of_col[:, None] == jnp.arange(HEAD_DIM)[None, :]).astype(BF16)
    slopes_row = jnp.zeros((1, HEAD_DIM), F32).at[0, :ATT_HEADS].set(slopes)
    tok = pl.BlockSpec((None, 1, ATT_W), lambda b: (b, 0, 0))
    const = lambda b: (0, 0)
    return pl.pallas_call(
        _attn_sample_kernel,
        out_shape=_sds((nb, 1, ATT_W), BF16),
        grid=(nb,),
        in_specs=[tok, tok, tok] + specs + [pl.BlockSpec((ATT_W, HEAD_DIM), const),
                                            pl.BlockSpec((HEAD_DIM, ATT_W), const),
                                            pl.BlockSpec((1, HEAD_DIM), const)],
        out_specs=tok,
        compiler_params=_cparams(("parallel",)),
        name="attn_sample",
    )(q, k_new, v_new, *views, eh, eh.T, slopes_row)


def _shifted(u, prev8, shift):
    row = lax.broadcasted_iota(jnp.int32, u.shape, 0)
    out = pltpu.roll(u, shift, 0)
    for s in range(shift):
        out = jnp.where(row == s, prev8[8 - shift + s:8 - shift + s + 1, :], out)
    return out


def _sconv_prompt_kernel(h_ref, gb_ref, gc_ref, hp_ref, gcp_ref, w_ref, o_ref):
    i = pl.program_id(1)
    u = gc_ref[...] * h_ref[...]
    up = jnp.where(i == 0, 0.0, gcp_ref[...] * hp_ref[...])
    w = w_ref[...]
    conv = w[0:1] * _shifted(u, up, 2) + w[1:2] * _shifted(u, up, 1) + w[2:3] * u
    o_ref[...] = (gb_ref[...] * conv).astype(o_ref.dtype)


def _sconv_prompt(proj, w, n_batch, seq, tm):
    sc = w.shape[1]
    nt = seq // tm
    cb = 3 * ATT_W // sc
    cur = lambda c: pl.BlockSpec((tm, sc), lambda b, i: (b * nt + i, cb + c))
    prev = lambda c: pl.BlockSpec((8, sc), lambda b, i: (jnp.maximum((b * nt + i) * (tm // 8) - 1, 0), cb + c))
    return pl.pallas_call(
        _sconv_prompt_kernel,
        out_shape=_sds((n_batch * seq, sc), BF16),
        grid=(n_batch, nt),
        in_specs=[cur(0), cur(1), cur(2), prev(0), prev(2), pl.BlockSpec((SC_WIDTH, sc), lambda b, i: (0, 0))],
        out_specs=pl.BlockSpec((tm, sc), lambda b, i: (b * nt + i, 0)),
        compiler_params=_cparams(("parallel", "parallel")),
        name="sconv_prompt",
    )(proj, proj, proj, proj, proj, w)


def _sconv_sample_kernel(h_ref, gb_ref, gc_ref, p0_ref, p1_ref, w_ref, o_ref, u_ref):
    u = gc_ref[...] * h_ref[...]
    w = w_ref[...]
    conv = w[0:1] * p0_ref[...] + w[1:2] * p1_ref[...] + w[2:3] * u
    o_ref[...] = (gb_ref[...] * conv).astype(o_ref.dtype)
    u_ref[...] = u


def _sconv_sample(h, gb, gc, p0, p1, w):
    nb, sc = h.shape
    return pl.pallas_call(
        _sconv_sample_kernel,
        out_shape=(_sds((nb, sc), BF16), _sds((nb, sc), F32)),
        name="sconv_sample",
    )(h, gb, gc, p0, p1, w)


def _qk_normalise(y, cb, n_qk_blocks):
    scale = jnp.where(cb < n_qk_blocks // 2, DN_DIM ** -0.5, 1.0)
    segs = []
    for s in range(y.shape[1] // DN_DIM):
        seg = y[:, s * DN_DIM:(s + 1) * DN_DIM]
        ss = jnp.sum(seg * seg, axis=-1, keepdims=True)
        segs.append(seg * (lax.rsqrt(ss + L2_EPS) * scale))
    return jnp.concatenate(segs, axis=1)


def _dconv_prompt_kernel(x_ref, xp_ref, w_ref, o_ref, *, n_qk_blocks):
    cb = pl.program_id(0)
    i = pl.program_id(2)
    x = x_ref[...]
    xp = jnp.where(i == 0, 0.0, xp_ref[...])
    w = w_ref[...]
    conv = (w[0:1] * _shifted(x, xp, 3) + w[1:2] * _shifted(x, xp, 2)
            + w[2:3] * _shifted(x, xp, 1) + w[3:4] * x)
    y = _silu(conv)

    @pl.when(cb < n_qk_blocks)
    def _():
        o_ref[...] = _qk_normalise(y, cb, n_qk_blocks)

    @pl.when(cb >= n_qk_blocks)
    def _():
        o_ref[...] = y


def _dconv_prompt(proj, w, n_batch, seq, tm, tc):
    nt = seq // tm
    ncb = DN_CONV_DIM // tc
    return pl.pallas_call(
        functools.partial(_dconv_prompt_kernel, n_qk_blocks=2 * DN_QK_W // tc),
        out_shape=_sds((n_batch * seq, DN_CONV_DIM), F32),
        grid=(ncb, n_batch, nt),
        in_specs=[pl.BlockSpec((tm, tc), lambda c, b, i: (b * nt + i, c)),
                  pl.BlockSpec((8, tc), lambda c, b, i: (jnp.maximum((b * nt + i) * (tm // 8) - 1, 0), c)),
                  pl.BlockSpec((DN_CONV_WIDTH, tc), lambda c, b, i: (0, c))],
        out_specs=pl.BlockSpec((tm, tc), lambda c, b, i: (b * nt + i, c)),
        compiler_params=_cparams(("parallel", "parallel", "parallel")),
        name="dconv_prompt",
    )(proj, proj, w)


def _dconv_sample_kernel(x_ref, p0_ref, p1_ref, p2_ref, w_ref, o_ref, *, n_qk_blocks):
    cb = pl.program_id(0)
    w = w_ref[...]
    conv = w[0:1] * p0_ref[...] + w[1:2] * p1_ref[...] + w[2:3] * p2_ref[...] + w[3:4] * x_ref[...]
    y = _silu(conv)

    @pl.when(cb < n_qk_blocks)
    def _():
        o_ref[...] = _qk_normalise(y, cb, n_qk_blocks)

    @pl.when(cb >= n_qk_blocks)
    def _():
        o_ref[...] = y


def _dconv_sample(x, p0, p1, p2, w, tc):
    nb = x.shape[0]
    blk = pl.BlockSpec((nb, tc), lambda c: (0, c))
    return pl.pallas_call(
        functools.partial(_dconv_sample_kernel, n_qk_blocks=2 * DN_QK_W // tc),
        out_shape=_sds((nb, DN_CONV_DIM), F32),
        grid=(DN_CONV_DIM // tc,),
        in_specs=[blk, blk, blk, blk, pl.BlockSpec((DN_CONV_WIDTH, tc), lambda c: (0, c))],
        out_specs=blk,
        compiler_params=_cparams(("parallel",)),
        name="dconv_sample",
    )(x, p0, p1, p2, w)


def _gated_rms(o, z, norm_w):
    return o * lax.rsqrt(jnp.mean(o * o, axis=-1, keepdims=True) + RMS_EPS) * norm_w * _silu(z)


def _delta_prompt_kernel(q_ref, k_ref, v_ref, z_ref, g_ref, beta_ref, nw_ref, o_ref, s_out_ref, s_ref,
                         *, n_chunks):
    c = pl.program_id(2)

    @pl.when(c == 0)
    def _():
        s_ref[...] = jnp.zeros_like(s_ref)

    cc = DN_CHUNK
    ii = lax.broadcasted_iota(jnp.int32, (cc, cc), 0)
    jj = lax.broadcasted_iota(jnp.int32, (cc, cc), 1)
    lower = (ii >= jj).astype(BF16)
    upper = (ii <= jj).astype(BF16)
    ones = jnp.ones((cc, cc), BF16)
    g_all = g_ref[...]
    beta_all = beta_ref[...]
    norm_w = nw_ref[...]

    for hq in range(DN_GROUP_QK):
        q = q_ref[:, hq * DN_DIM:(hq + 1) * DN_DIM]
        k = k_ref[:, hq * DN_DIM:(hq + 1) * DN_DIM]
        k_b = k.astype(BF16)
        qk = _dot_nt(q.astype(BF16), k_b)
        for j in range(2):
            hv = 2 * hq + j
            g_col = g_all[:, hv:hv + 1]
            beta_col = beta_all[:, hv:hv + 1]
            gc = _dot_exact_lhs(lower, jnp.broadcast_to(g_col, (cc, DN_DIM)))
            gr = _dot_exact_lhs(ones, jnp.broadcast_to(g_col, (cc, cc)) * upper.astype(F32))
            decay = jnp.where(ii >= jj, jnp.exp(jnp.minimum(gc[:, :cc] - gr, 0.0)), 0.0)
            eg = jnp.exp(gc)
            kbeta = k * beta_col
            a_mat = jnp.where(ii > jj, _dot_nt(kbeta.astype(BF16), k_b) * decay, 0.0)
            y = jnp.concatenate([v_ref[:, hv * DN_DIM:(hv + 1) * DN_DIM] * beta_col, kbeta * eg], axis=1)
            npow = -a_mat
            for step in range(6):
                y = y + _dot3(npow, y)
                if step < 5:
                    npow = _dot3(npow, npow)
            u = y[:, :DN_DIM]
            w = y[:, DN_DIM:]
            s_old = s_ref[hv]
            s_b = s_old.astype(BF16)
            v_new = u - _dot(w.astype(BF16), s_b)
            v_new_b = v_new.astype(BF16)
            o = _dot((q * eg).astype(BF16), s_b) + _dot((qk * decay).astype(BF16), v_new_b)
            g_last = gc[cc - 1:cc, :]
            k_dec = k * jnp.exp(g_last - gc)
            s_ref[hv] = s_old * jnp.exp(g_last) + _dot_tn(k_dec.astype(BF16), v_new_b)
            z = z_ref[:, hv * DN_DIM:(hv + 1) * DN_DIM]
            o_ref[:, hv * DN_DIM:(hv + 1) * DN_DIM] = _gated_rms(o, z, norm_w).astype(o_ref.dtype)

    @pl.when(c == n_chunks - 1)
    def _():
        s_out_ref[...] = s_ref[...]


def _delta_prompt(qkv, proj, g, beta, norm_w, n_batch, seq):
    assert seq % DN_CHUNK == 0
    nc = seq // DN_CHUNK
    n_hg = DN_QK_HEADS // DN_GROUP_QK
    wq = DN_GROUP_QK * DN_DIM
    wv = DN_GROUP_V * DN_DIM
    row = lambda b, hg, c: b * nc + c
    return pl.pallas_call(
        functools.partial(_delta_prompt_kernel, n_chunks=nc),
        out_shape=(_sds((n_batch * seq, DN_VW), BF16), _sds((n_batch, DN_V_HEADS, DN_DIM, DN_DIM), F32)),
        grid=(n_batch, n_hg, nc),
        in_specs=[pl.BlockSpec((DN_CHUNK, wq), lambda b, hg, c: (row(b, hg, c), hg)),
                  pl.BlockSpec((DN_CHUNK, wq), lambda b, hg, c: (row(b, hg, c), n_hg + hg)),
                  pl.BlockSpec((DN_CHUNK, wv), lambda b, hg, c: (row(b, hg, c), 2 * DN_QK_W // wv + hg)),
                  pl.BlockSpec((DN_CHUNK, wv), lambda b, hg, c: (row(b, hg, c), DN_CONV_DIM // wv + hg)),
                  pl.BlockSpec((None, DN_CHUNK, DN_GROUP_V), lambda b, hg, c: (hg, row(b, hg, c), 0)),
                  pl.BlockSpec((None, DN_CHUNK, DN_GROUP_V), lambda b, hg, c: (hg, row(b, hg, c), 0)),
                  pl.BlockSpec((1, DN_DIM), lambda b, hg, c: (0, 0))],
        out_specs=(pl.BlockSpec((DN_CHUNK, wv), lambda b, hg, c: (row(b, hg, c), hg)),
                   pl.BlockSpec((None, DN_GROUP_V, DN_DIM, DN_DIM), lambda b, hg, c: (b, hg, 0, 0))),
        scratch_shapes=[pltpu.VMEM((DN_GROUP_V, DN_DIM, DN_DIM), F32)],
        compiler_params=_cparams(("parallel", "parallel", "arbitrary")),
        name="delta_prompt",
    )(qkv, qkv, qkv, proj, g, beta, norm_w.reshape(1, DN_DIM))


def _delta_sample_kernel(qt_ref, kt_ref, v_ref, z_ref, g_ref, beta_ref, nw_ref, s_ref, o_ref, s_out_ref):
    norm_w = nw_ref[...]
    qt = qt_ref[...]
    kt = kt_ref[...]
    for hv in range(DN_V_HEADS):
        hq = hv // 2
        k_col = jnp.broadcast_to(kt[:, hq:hq + 1], (DN_DIM, DN_DIM))
        q_col = jnp.broadcast_to(qt[:, hq:hq + 1], (DN_DIM, DN_DIM))
        eg = jnp.exp(g_ref[hv:hv + 1, :])
        beta = beta_ref[hv:hv + 1, :]
        s = s_ref[hv]
        sk = jnp.sum(s * k_col, axis=0, keepdims=True)
        delta = beta * (v_ref[hv:hv + 1, :] - eg * sk)
        s_new = s * eg + k_col * delta
        s_out_ref[hv] = s_new
        o = jnp.sum(s_new * q_col, axis=0, keepdims=True)
        o_ref[hv:hv + 1, :] = _gated_rms(o, z_ref[hv:hv + 1, :], norm_w).astype(o_ref.dtype)


def _delta_sample(qt, kt, v, z, g_b, beta_b, norm_w, s0):
    nb = v.shape[0]
    per_b3 = lambda shape: pl.BlockSpec((None,) + shape, lambda b: (b, 0, 0))
    st = pl.BlockSpec((None, DN_V_HEADS, DN_DIM, DN_DIM), lambda b: (b, 0, 0, 0))
    return pl.pallas_call(
        _delta_sample_kernel,
        out_shape=(_sds((nb, DN_V_HEADS, DN_DIM), BF16), _sds(s0.shape, F32)),
        grid=(nb,),
        in_specs=[per_b3((DN_DIM, DN_QK_HEADS)), per_b3((DN_DIM, DN_QK_HEADS)),
                  per_b3((DN_V_HEADS, DN_DIM)), per_b3((DN_V_HEADS, DN_DIM)),
                  per_b3((DN_V_HEADS, DN_DIM)), per_b3((DN_V_HEADS, DN_DIM)),
                  pl.BlockSpec((1, DN_DIM), lambda b: (0, 0)), st],
        out_specs=(per_b3((DN_V_HEADS, DN_DIM)), st),
        compiler_params=_cparams(("parallel",)),
        name="delta_sample",
    )(qt, kt, v, z, g_b, beta_b, norm_w.reshape(1, DN_DIM), s0)


def _mixer_layer(xf, xb, w_in, sconv_w, w_out, k_buf, v_buf, sconv_prev, n_batch, seq, g1, b1):
    tp = n_batch * seq
    proj = _matmul(xb, w_in.astype(BF16), 832, 512)
    slopes = jnp.exp2(-8.0 * (jnp.arange(ATT_HEADS, dtype=F32) + 1.0) / ATT_HEADS)
    att_p = _attn_prompt(proj, slopes, n_batch, seq)
    ns = xf.shape[0] - tp
    ps = proj[tp:]
    q_s, k_s, v_s = (ps[:, i * ATT_W:(i + 1) * ATT_W] for i in range(3))
    att_s = _attn_sample(q_s.reshape(ns, 1, ATT_W), k_s.reshape(ns, 1, ATT_W), v_s.reshape(ns, 1, ATT_W),
                         k_buf.reshape(ns, -1, ATT_W), v_buf.reshape(ns, -1, ATT_W), slopes)
    sc = sconv_w.shape[1]
    gc_p = _sconv_prompt(proj, sconv_w, n_batch, seq, 512)
    h_s, gb_s, gcs_s = (ps[:, 3 * ATT_W + i * sc:3 * ATT_W + (i + 1) * sc] for i in range(3))
    gc_s, u_s = _sconv_sample(h_s, gb_s, gcs_s, sconv_prev[:, 0], sconv_prev[:, 1], sconv_w)
    mixed = jnp.concatenate([jnp.concatenate([att_p, att_s.reshape(ns, ATT_W)], axis=0),
                             jnp.concatenate([gc_p, gc_s], axis=0)], axis=1)
    x1f, x1b = _matmul_residual_ln(mixed, w_out.astype(BF16), xf, g1, b1, 320, w_out.shape[0])

    keep = min(SLAB, seq)
    pp = proj[:tp].reshape(n_batch, seq, -1)
    k_rows_p = pp[:, seq - keep:, ATT_W:2 * ATT_W].reshape(n_batch, keep, ATT_HEADS, HEAD_DIM)
    v_rows_p = pp[:, seq - keep:, 2 * ATT_W:3 * ATT_W].reshape(n_batch, keep, ATT_HEADS, HEAD_DIM)
    u_tail = pp[:, seq - (SC_WIDTH - 1):, 3 * ATT_W + 2 * sc:] * pp[:, seq - (SC_WIDTH - 1):, 3 * ATT_W:3 * ATT_W + sc]
    sconv_new_s = jnp.stack([sconv_prev[:, 1], u_s], axis=1)
    outs_p = (k_rows_p, v_rows_p, u_tail)
    outs_s = (k_s.reshape(ns, 1, ATT_HEADS, HEAD_DIM), v_s.reshape(ns, 1, ATT_HEADS, HEAD_DIM), sconv_new_s)
    return x1f, x1b, outs_p, outs_s


def _delta_layer(xf, xb, w_in, conv_w, a_log, dt_bias, norm_w, w_out, conv_prev, s0, n_batch, seq, g1, b1):
    tp = n_batch * seq
    ns = xf.shape[0] - tp
    n_main = DN_CONV_DIM + DN_VW
    proj = _matmul(xb, w_in[:, :n_main].astype(BF16), 832, 512)
    ba = _matmul(xb, w_in[:, n_main:].astype(BF16), 832, 2 * DN_V_HEADS)
    beta = jax.nn.sigmoid(ba[:, :DN_V_HEADS])
    g = -jnp.exp(a_log.astype(F32)) * jax.nn.softplus(ba[:, DN_V_HEADS:] + dt_bias.astype(F32))

    qkv_p = _dconv_prompt(proj, conv_w, n_batch, seq, 512, 512)
    n_vg = DN_V_HEADS // DN_GROUP_V
    grp = lambda a: a[:tp].reshape(tp, n_vg, DN_GROUP_V).transpose(1, 0, 2)
    o_p, s_p = _delta_prompt(qkv_p, proj, grp(g), grp(beta), norm_w, n_batch, seq)

    ps = proj[tp:]
    qkv_s = _dconv_sample(ps[:, :DN_CONV_DIM], conv_prev[:, 0], conv_prev[:, 1], conv_prev[:, 2], conv_w, 512)
    qt = qkv_s[:, :DN_QK_W].reshape(ns, DN_QK_HEADS, DN_DIM).transpose(0, 2, 1)
    kt = qkv_s[:, DN_QK_W:2 * DN_QK_W].reshape(ns, DN_QK_HEADS, DN_DIM).transpose(0, 2, 1)
    v_s = qkv_s[:, 2 * DN_QK_W:].reshape(ns, DN_V_HEADS, DN_DIM)
    z_s = ps[:, DN_CONV_DIM:].reshape(ns, DN_V_HEADS, DN_DIM)
    lanes = lambda a: jnp.broadcast_to(a[tp:, :, None], (ns, DN_V_HEADS, DN_DIM))
    o_s, s_s = _delta_sample(qt, kt, v_s, z_s, lanes(g), lanes(beta), norm_w, s0)

    og = jnp.concatenate([o_p, o_s.reshape(ns, DN_VW)], axis=0)
    x1f, x1b = _matmul_residual_ln(og, w_out.astype(BF16), xf, g1, b1, 320, 2048)

    dconv_new_p = proj[:tp].reshape(n_batch, seq, -1)[:, seq - (DN_CONV_WIDTH - 1):, :DN_CONV_DIM]
    dconv_new_s = jnp.concatenate([conv_prev[:, 1:], ps[:, None, :DN_CONV_DIM]], axis=1)
    return x1f, x1b, (dconv_new_p, s_p), (dconv_new_s, s_s)


def _moe_block(x1f, x1b, p_b, w_router, e_bias, w_gate, w_up, w_down, ws_gate, ws_up, ws_down,
               ple_w_proj, ple_w_gate, g2, b2):
    t, d = x1f.shape
    logits = _matmul(x1b, w_router.astype(BF16), 832, N_EXPERTS)
    e_idx, wts = _route(logits, e_bias)
    row_tok, tile_expert, tile_valid, dest_of_assign = _dispatch_plan(e_idx)
    y_rows = _moe_experts(x1b[row_tok], tile_expert, tile_valid, w_gate, w_up, w_down)
    y_assign = y_rows[dest_of_assign].reshape(t, TOP_K, d)
    routed = jnp.einsum('tk,tkd->td', wts.astype(BF16).astype(F32), y_assign.astype(F32))
    return _final_block(x1f, x1b, routed, p_b, ws_gate.astype(BF16), ws_up.astype(BF16),
                        ws_down.astype(BF16), ple_w_gate.astype(BF16), ple_w_proj.astype(BF16), g2, b2, 320)


def kernel(x_prompt, x_sample, cache_attn_k, cache_attn_v, state_sconv, state_dconv, state_delta,
           p_prompt, p_sample, mix_w_in, mix_sconv_w, mix_w_out, dn_w_in, dn_conv_w, dn_a_log,
           dn_dt_bias, dn_norm_w, dn_w_out, ln1_g, ln1_b, ln2_g, ln2_b, moe_w_router, moe_e_bias,
           moe_w_gate, moe_w_up, moe_w_down, moe_ws_gate, moe_ws_up, moe_ws_down, ple_w_proj,
           ple_w_gate):
    n_batch, seq, d = x_prompt.shape
    ns = x_sample.shape[0]
    assert x_sample.shape[1] == 1
    tp = n_batch * seq
    xf = jnp.concatenate([x_prompt.reshape(tp, d), x_sample.reshape(ns, d)], axis=0)
    xb = xf.astype(BF16)
    p_all = jnp.concatenate([p_prompt.reshape(DEPTH, tp, -1), p_sample.reshape(DEPTH, ns, -1)],
                            axis=1).astype(BF16)

    mix_p, mix_s, dn_p, dn_s = [], [], [], []
    for i in range(DEPTH):
        j = i // 2
        if i % 2 == 0:
            x1f, x1b, o_p, o_s = _mixer_layer(
                xf, xb, mix_w_in[j], mix_sconv_w[j], mix_w_out[j], cache_attn_k[j], cache_attn_v[j],
                state_sconv[j], n_batch, seq, ln1_g[i], ln1_b[i])
            mix_p.append(o_p)
            mix_s.append(o_s)
        else:
            x1f, x1b, o_p, o_s = _delta_layer(
                xf, xb, dn_w_in[j], dn_conv_w[j], dn_a_log[j], dn_dt_bias[j], dn_norm_w[j], dn_w_out[j],
                state_dconv[j], state_delta[j], n_batch, seq, ln1_g[i], ln1_b[i])
            dn_p.append(o_p)
            dn_s.append(o_s)
        xf, xb = _moe_block(x1f, x1b, p_all[i], moe_w_router[i], moe_e_bias[i], moe_w_gate[i],
                            moe_w_up[i], moe_w_down[i], moe_ws_gate[i], moe_ws_up[i], moe_ws_down[i],
                            ple_w_proj[i], ple_w_gate[i], ln2_g[i], ln2_b[i])

    stack = lambda parts, idx: jnp.stack([p[idx] for p in parts])
    y_prompt = xf[:tp].reshape(n_batch, seq, d)
    y_sample = xf[tp:].reshape(ns, 1, d)
    return (y_prompt, y_sample,
            stack(mix_p, 0), stack(mix_p, 1), stack(mix_p, 2), stack(dn_p, 0), stack(dn_p, 1),
            stack(mix_s, 0), stack(mix_s, 1), stack(mix_s, 2), stack(dn_s, 0), stack(dn_s, 1))
```

```python
import functools

import jax
import jax.numpy as jnp
from jax import lax
from jax.experimental import pallas as pl
from jax.experimental.pallas import tpu as pltpu

F32 = jnp.float32
BF16 = jnp.bfloat16

DEPTH = 2
HEAD_DIM = 128
ATT_HEADS = 12
ATT_W = ATT_HEADS * HEAD_DIM
SC_WIDTH = 3
DILATIONS = (1, 4, 16)
BAND = 128
SLAB = BAND * DILATIONS[-1]
ATT_SCALE = HEAD_DIM ** -0.5
DN_QK_HEADS = 16
DN_V_HEADS = 32
DN_DIM = 128
DN_QK_W = DN_QK_HEADS * DN_DIM
DN_VW = DN_V_HEADS * DN_DIM
DN_CONV_DIM = 2 * DN_QK_W + DN_VW
DN_CONV_WIDTH = 4
DN_CHUNK = 64
DN_GROUP_QK = 4
DN_GROUP_V = 2 * DN_GROUP_QK
N_EXPERTS = 64
N_GROUPS = 8
TOPK_GROUPS = 4
TOP_K = 8
ROUTED_SCALE = 2.5
ALPHA = (2 * DEPTH) ** 0.25
LN_EPS = 1e-5
RMS_EPS = 1e-6
L2_EPS = 1e-6
NEG = -1e30

MOE_TM = 256
VMEM_LIMIT = 56 * 1024 * 1024


def _sds(shape, dtype):
    return jax.ShapeDtypeStruct(shape, dtype)


def _cparams(sem):
    return pltpu.CompilerParams(dimension_semantics=sem, vmem_limit_bytes=VMEM_LIMIT)


def _silu(x):
    return x * jax.nn.sigmoid(x)


def _layer_norm_rows(v, g, b):
    mu = jnp.mean(v, axis=-1, keepdims=True)
    c = v - mu
    var = jnp.mean(c * c, axis=-1, keepdims=True)
    return c * lax.rsqrt(var + LN_EPS) * g + b


def _split3(x):
    hi = x.astype(BF16)
    r1 = x - hi.astype(F32)
    mid = r1.astype(BF16)
    lo = (r1 - mid.astype(F32)).astype(BF16)
    return hi, mid, lo


def _dot(a, b):
    return jnp.dot(a, b, preferred_element_type=F32)


def _dot_nt(a, b):
    return lax.dot_general(a, b, (((1,), (1,)), ((), ())), preferred_element_type=F32)


def _dot_tn(a, b):
    return lax.dot_general(a, b, (((0,), (0,)), ((), ())), preferred_element_type=F32)


def _dot_exact_lhs(sel_bf16, x):
    hi, mid, lo = _split3(x)
    return _dot(sel_bf16, hi) + _dot(sel_bf16, mid) + _dot(sel_bf16, lo)


def _dot_exact_rhs(x, sel_bf16):
    hi, mid, lo = _split3(x)
    return _dot(hi, sel_bf16) + _dot(mid, sel_bf16) + _dot(lo, sel_bf16)


def _dot3(a, b):
    a_hi = a.astype(BF16)
    a_lo = (a - a_hi.astype(F32)).astype(BF16)
    b_hi = b.astype(BF16)
    b_lo = (b - b_hi.astype(F32)).astype(BF16)
    return _dot(a_hi, b_hi) + _dot(a_hi, b_lo) + _dot(a_lo, b_hi)


def _mm_kernel(x_ref, w_ref, o_ref):
    o_ref[...] = _dot(x_ref[...], w_ref[...])


def _matmul(x, w, tm, tn):
    m, k = x.shape
    n = w.shape[1]
    assert m % tm == 0 and n % tn == 0
    return pl.pallas_call(
        _mm_kernel,
        out_shape=_sds((m, n), F32),
        grid=(m // tm, n // tn),
        in_specs=[pl.BlockSpec((tm, k), lambda i, j: (i, 0)),
                  pl.BlockSpec((k, tn), lambda i, j: (0, j))],
        out_specs=pl.BlockSpec((tm, tn), lambda i, j: (i, j)),
        compiler_params=_cparams(("parallel", "parallel")),
        name="matmul",
    )(x, w)


def _mm_ln_kernel(a_ref, w_ref, x_ref, g_ref, b_ref, of_ref, ob_ref, acc_ref, *, nk):
    kk = pl.program_id(1)
    part = _dot(a_ref[...], w_ref[...])

    @pl.when(kk == 0)
    def _():
        acc_ref[...] = part

    @pl.when(kk > 0)
    def _():
        acc_ref[...] += part

    @pl.when(kk == nk - 1)
    def _():
        y = _layer_norm_rows(ALPHA * x_ref[...] + acc_ref[...], g_ref[...], b_ref[...])
        of_ref[...] = y
        ob_ref[...] = y.astype(BF16)


def _matmul_residual_ln(a, w, x, g, b, tm, tk):
    m, k = a.shape
    d = w.shape[1]
    assert m % tm == 0 and k % tk == 0
    nk = k // tk
    return pl.pallas_call(
        functools.partial(_mm_ln_kernel, nk=nk),
        out_shape=(_sds((m, d), F32), _sds((m, d), BF16)),
        grid=(m // tm, nk),
        in_specs=[pl.BlockSpec((tm, tk), lambda i, kk: (i, kk)),
                  pl.BlockSpec((tk, d), lambda i, kk: (kk, 0)),
                  pl.BlockSpec((tm, d), lambda i, kk: (i, 0)),
                  pl.BlockSpec((1, d), lambda i, kk: (0, 0)),
                  pl.BlockSpec((1, d), lambda i, kk: (0, 0))],
        out_specs=(pl.BlockSpec((tm, d), lambda i, kk: (i, 0)),
                   pl.BlockSpec((tm, d), lambda i, kk: (i, 0))),
        scratch_shapes=[pltpu.VMEM((tm, d), F32)],
        compiler_params=_cparams(("parallel", "arbitrary")),
        name="matmul_residual_ln",
    )(a, w, x, g.reshape(1, d), b.reshape(1, d))


def _moe_kernel(te_ref, tv_ref, x_ref, wg_ref, wu_ref, wd_ref, o_ref, wg_s, wu_s, wd_s):
    i = pl.program_id(0)
    e = te_ref[i]
    prev = te_ref[jnp.maximum(i - 1, 0)]

    @pl.when(jnp.logical_or(i == 0, e != prev))
    def _():
        wg_s[...] = wg_ref[...].astype(BF16)
        wu_s[...] = wu_ref[...].astype(BF16)
        wd_s[...] = wd_ref[...].astype(BF16)

    @pl.when(tv_ref[i] > 0)
    def _():
        x = x_ref[...]
        a = _dot(x, wg_s[...])
        u = _dot(x, wu_s[...])
        o_ref[...] = _dot((_silu(a) * u).astype(BF16), wd_s[...]).astype(o_ref.dtype)

    @pl.when(tv_ref[i] == 0)
    def _():
        o_ref[...] = jnp.zeros_like(o_ref)


def _moe_experts(x_rows, tile_expert, tile_valid, w_gate, w_up, w_down):
    n_rows, d = x_rows.shape
    n_tiles = n_rows // MOE_TM
    de = w_gate.shape[-1]
    grid_spec = pltpu.PrefetchScalarGridSpec(
        num_scalar_prefetch=2,
        grid=(n_tiles,),
        in_specs=[pl.BlockSpec((MOE_TM, d), lambda i, te, tv: (i, 0)),
                  pl.BlockSpec((None, d, de), lambda i, te, tv: (te[i], 0, 0)),
                  pl.BlockSpec((None, d, de), lambda i, te, tv: (te[i], 0, 0)),
                  pl.BlockSpec((None, de, d), lambda i, te, tv: (te[i], 0, 0))],
        out_specs=pl.BlockSpec((MOE_TM, d), lambda i, te, tv: (i, 0)),
        scratch_shapes=[pltpu.VMEM((d, de), BF16), pltpu.VMEM((d, de), BF16),
                        pltpu.VMEM((de, d), BF16)],
    )
    return pl.pallas_call(
        _moe_kernel,
        out_shape=_sds((n_rows, d), BF16),
        grid_spec=grid_spec,
        compiler_params=_cparams(("arbitrary",)),
        name="moe_experts",
    )(tile_expert, tile_valid, x_rows, w_gate, w_up, w_down)


def _route(logits, e_bias):
    t = logits.shape[0]
    scores = jax.nn.sigmoid(logits)
    sel = scores + e_bias.astype(F32)
    per_group = sel.reshape(t, N_GROUPS, N_EXPERTS // N_GROUPS)
    group_score = lax.top_k(per_group, 2)[0].sum(-1)
    _, g_idx = lax.top_k(group_score, TOPK_GROUPS)
    g_mask = jax.nn.one_hot(g_idx, N_GROUPS, dtype=F32).sum(1) > 0
    e_mask = jnp.repeat(g_mask, N_EXPERTS // N_GROUPS, axis=1)
    _, e_idx = lax.top_k(jnp.where(e_mask, sel, -jnp.inf), TOP_K)
    wts = jnp.take_along_axis(scores, e_idx, axis=1)
    wts = wts / wts.sum(-1, keepdims=True) * ROUTED_SCALE
    return e_idx, wts


def _dispatch_plan(e_idx):
    t = e_idx.shape[0]
    n_assign = t * TOP_K
    member = (e_idx[:, :, None] == jnp.arange(N_EXPERTS)[None, None, :]).any(axis=1).astype(jnp.int32)
    running = jnp.cumsum(member, axis=0)
    counts = running[-1]
    rank = jnp.take_along_axis(running, e_idx, axis=1) - 1
    padded = (counts + MOE_TM - 1) // MOE_TM * MOE_TM
    pad_end = jnp.cumsum(padded)
    pad_start = pad_end - padded
    dest_of_assign = (pad_start[e_idx] + rank).astype(jnp.int32).reshape(-1)
    n_tiles = -(-(n_assign + N_EXPERTS * (MOE_TM - 1)) // MOE_TM)
    tok_of_assign = jnp.repeat(jnp.arange(t, dtype=jnp.int32), TOP_K)
    row_tok = jnp.zeros((n_tiles * MOE_TM,), jnp.int32).at[dest_of_assign].set(tok_of_assign)
    tile_start = jnp.arange(n_tiles) * MOE_TM
    tile_expert = jnp.minimum((tile_start[:, None] >= pad_end[None, :]).sum(axis=1),
                              N_EXPERTS - 1).astype(jnp.int32)
    tile_valid = (tile_start < pad_end[-1]).astype(jnp.int32)
    last_used = tile_expert[jnp.maximum(pad_end[-1] // MOE_TM - 1, 0)]
    tile_expert = jnp.where(tile_valid > 0, tile_expert, last_used)
    return row_tok, tile_expert, tile_valid, dest_of_assign


def _final_kernel(xf_ref, xb_ref, h_ref, p_ref, wsg_ref, wsu_ref, wsd_ref, wpg_ref, wpp_ref,
                  g_ref, b_ref, of_ref, ob_ref):
    x = xb_ref[...]
    a = _dot(x, wsg_ref[...])
    u = _dot(x, wsu_ref[...])
    shared = _dot((_silu(a) * u).astype(BF16), wsd_ref[...])
    gate = jax.nn.sigmoid(_dot(x, wpg_ref[...]))
    ple = gate * _dot(p_ref[...], wpp_ref[...])
    v = ALPHA * xf_ref[...] + (h_ref[...] + shared) + ple
    y = _layer_norm_rows(v, g_ref[...], b_ref[...])
    of_ref[...] = y
    ob_ref[...] = y.astype(BF16)


def _final_block(xf, xb, h, p, wsg, wsu, wsd, wpg, wpp, g, b, tm):
    m, d = xf.shape
    de = wsg.shape[1]
    dp = p.shape[1]
    assert m % tm == 0
    row = lambda i: (i, 0)
    const = lambda i: (0, 0)
    once = dict(pipeline_mode=pl.Buffered(1))
    return pl.pallas_call(
        _final_kernel,
        out_shape=(_sds((m, d), F32), _sds((m, d), BF16)),
        grid=(m // tm,),
        in_specs=[pl.BlockSpec((tm, d), row), pl.BlockSpec((tm, d), row),
                  pl.BlockSpec((tm, d), row), pl.BlockSpec((tm, dp), row),
                  pl.BlockSpec((d, de), const, **once), pl.BlockSpec((d, de), const, **once),
                  pl.BlockSpec((de, d), const, **once), pl.BlockSpec((d, d), const, **once),
                  pl.BlockSpec((dp, d), const, **once),
                  pl.BlockSpec((1, d), const), pl.BlockSpec((1, d), const)],
        out_specs=(pl.BlockSpec((tm, d), row), pl.BlockSpec((tm, d), row)),
        compiler_params=_cparams(("parallel",)),
        name="shared_ple_ln",
    )(xf, xb, h, p, wsg, wsu, wsd, wpg, wpp, g.reshape(1, d), b.reshape(1, d))


def _attn_prompt_kernel(slopes_ref, q_ref, k_ref, v_ref, o_ref, kk, vv, m_s, l_s, acc_s):
    h = pl.program_id(1)
    sl = pl.program_id(2)

    @pl.when(sl == 0)
    def _():
        kk[0:SLAB, :] = jnp.zeros((SLAB, HEAD_DIM), F32)
        vv[0:SLAB, :] = jnp.zeros((SLAB, HEAD_DIM), F32)

    kk[SLAB:2 * SLAB, :] = k_ref[...]
    vv[SLAB:2 * SLAB, :] = v_ref[...]

    slope = slopes_ref[h]
    ii = lax.broadcasted_iota(jnp.int32, (BAND, BAND), 0)
    jj = lax.broadcasted_iota(jnp.int32, (BAND, BAND), 1)
    dist_prev = (ii + BAND - jj).astype(F32)
    dist_cur = (ii - jj).astype(F32)

    for di, d in enumerate(DILATIONS):
        c = slope * float(d)
        bias_prev = jnp.where(jj >= ii, -c * dist_prev, NEG)
        bias_cur = jnp.where(jj <= ii, -c * dist_cur, NEG)

        def unit(u, carry, di=di, d=d, bias_prev=bias_prev, bias_cur=bias_cur):
            sp = u // d
            r = u - sp * d
            base = sp * (BAND * d) + r

            def rows(start):
                if d == 1:
                    return pl.ds(start, BAND)
                return pl.ds(start, BAND, stride=d)

            q = q_ref[rows(base), :].astype(BF16)
            kc = kk[rows(base + SLAB), :].astype(BF16)
            vc = vv[rows(base + SLAB), :].astype(BF16)
            kp = kk[rows(base + SLAB - BAND * d), :].astype(BF16)
            vp = vv[rows(base + SLAB - BAND * d), :].astype(BF16)
            pen = jnp.where(jnp.logical_and(sl == 0, sp == 0), NEG, 0.0)
            s_p = _dot_nt(q, kp) * ATT_SCALE + (bias_prev + pen)
            s_c = _dot_nt(q, kc) * ATT_SCALE + bias_cur
            m = jnp.maximum(jnp.max(s_p, axis=-1, keepdims=True), jnp.max(s_c, axis=-1, keepdims=True))
            p_p = jnp.exp(s_p - m)
            p_c = jnp.exp(s_c - m)
            l = jnp.sum(p_p, axis=-1, keepdims=True) + jnp.sum(p_c, axis=-1, keepdims=True)
            m_s[di, rows(base), :] = jnp.broadcast_to(m, (BAND, HEAD_DIM))
            l_s[di, rows(base), :] = jnp.broadcast_to(l, (BAND, HEAD_DIM))
            acc_s[di, rows(base), :] = _dot(p_p.astype(BF16), vp) + _dot(p_c.astype(BF16), vc)
            return carry

        lax.fori_loop(0, SLAB // BAND, unit, 0)

    def merge(t, carry):
        rows = pl.ds(pl.multiple_of(t * BAND, BAND), BAND)
        m_max = jnp.maximum(jnp.maximum(m_s[0, rows, :], m_s[1, rows, :]), m_s[2, rows, :])
        w0 = jnp.exp(m_s[0, rows, :] - m_max)
        num = w0 * acc_s[0, rows, :]
        den = w0 * l_s[0, rows, :]
        for di in range(1, len(DILATIONS)):
            w = jnp.exp(m_s[di, rows, :] - m_max)
            num = num + w * acc_s[di, rows, :]
            den = den + w * l_s[di, rows, :]
        o_ref[rows, :] = (num / den).astype(o_ref.dtype)
        return carry

    lax.fori_loop(0, SLAB // BAND, merge, 0)
    kk[0:SLAB, :] = kk[SLAB:2 * SLAB, :]
    vv[0:SLAB, :] = vv[SLAB:2 * SLAB, :]


def _attn_prompt(proj, slopes, n_batch, seq):
    assert seq % SLAB == 0
    ns = seq // SLAB
    grid_spec = pltpu.PrefetchScalarGridSpec(
        num_scalar_prefetch=1,
        grid=(n_batch, ATT_HEADS, ns),
        in_specs=[pl.BlockSpec((SLAB, HEAD_DIM), lambda b, h, s, sl: (b * ns + s, h)),
                  pl.BlockSpec((SLAB, HEAD_DIM), lambda b, h, s, sl: (b * ns + s, ATT_HEADS + h)),
                  pl.BlockSpec((SLAB, HEAD_DIM), lambda b, h, s, sl: (b * ns + s, 2 * ATT_HEADS + h))],
        out_specs=pl.BlockSpec((SLAB, HEAD_DIM), lambda b, h, s, sl: (b * ns + s, h)),
        scratch_shapes=[pltpu.VMEM((2 * SLAB, HEAD_DIM), F32), pltpu.VMEM((2 * SLAB, HEAD_DIM), F32),
                        pltpu.VMEM((len(DILATIONS), SLAB, HEAD_DIM), F32),
                        pltpu.VMEM((len(DILATIONS), SLAB, HEAD_DIM), F32),
                        pltpu.VMEM((len(DILATIONS), SLAB, HEAD_DIM), F32)],
    )
    return pl.pallas_call(
        _attn_prompt_kernel,
        out_shape=_sds((n_batch * seq, ATT_W), BF16),
        grid_spec=grid_spec,
        compiler_params=_cparams(("parallel", "parallel", "arbitrary")),
        name="attn_prompt",
    )(slopes, proj, proj, proj)


def _window_copy(view_hbm, dst, sems, b, slot, di, which):
    groups = view_hbm.shape[1]
    return pltpu.make_async_copy(view_hbm.at[b, pl.ds(groups - BAND, BAND), 0], dst.at[slot, di],
                                 sems.at[slot, which, di])


def _attn_sample_kernel(q_ref, kn_ref, vn_ref, slopes_ref, k1_hbm, k4_hbm, k16_hbm, v1_hbm, v4_hbm, v16_hbm,
                        o_ref, k_win, v_win, sems):
    b = pl.program_id(0)
    nb = pl.num_programs(0)
    slot = lax.rem(b, 2)
    k_views = (k1_hbm, k4_hbm, k16_hbm)
    v_views = (v1_hbm, v4_hbm, v16_hbm)

    def fetch(bb, sl, start):
        for di in range(len(DILATIONS)):
            for which, (src, dst) in enumerate(((k_views[di], k_win), (v_views[di], v_win))):
                cp = _window_copy(src, dst, sems, bb, sl, di, which)
                if start:
                    cp.start()
                else:
                    cp.wait()

    @pl.when(b == 0)
    def _():
        fetch(0, 0, True)

    @pl.when(b + 1 < nb)
    def _():
        fetch(b + 1, 1 - slot, True)

    fetch(b, slot, False)

    q = q_ref[...]
    slopes = slopes_ref[...]
    s_self = jnp.sum(q * kn_ref[...], axis=-1, keepdims=True) * ATT_SCALE
    steps = (BAND - lax.broadcasted_iota(jnp.int32, (BAND, 1, 1), 0)).astype(F32)
    parts = []
    for di, d in enumerate(DILATIONS):
        sc = (jnp.sum(k_win[slot, di] * q, axis=-1, keepdims=True) * ATT_SCALE
              - slopes * (steps * float(d)))
        m_d = jnp.maximum(jnp.max(sc, axis=0), s_self)
        p = jnp.exp(sc - m_d)
        p_self = jnp.exp(s_self - m_d)
        l_d = jnp.sum(p, axis=0) + p_self
        acc = jnp.sum(p * v_win[slot, di], axis=0) + p_self * vn_ref[...]
        parts.append((m_d, l_d, acc))
    m_max = jnp.maximum(jnp.maximum(parts[0][0], parts[1][0]), parts[2][0])
    num = jnp.zeros(q.shape, F32)
    den = jnp.zeros(s_self.shape, F32)
    for m_d, l_d, acc in parts:
        w = jnp.exp(m_d - m_max)
        num = num + w * acc
        den = den + w * l_d
    o_ref[...] = (num / den).astype(o_ref.dtype)


def _attn_sample(q, k_new, v_new, k_buf, v_buf, slopes):
    nb, n_buf, nh, e = k_buf.shape
    assert n_buf >= BAND * DILATIONS[-1] and all(n_buf % d == 0 for d in DILATIONS)
    tok = pl.BlockSpec((None, nh, e), lambda b: (b, 0, 0))
    hbm = pl.BlockSpec(memory_space=pl.ANY)
    win = pltpu.VMEM((2, len(DILATIONS), BAND, nh, e), F32)
    views = [buf.reshape(nb, n_buf // d, d, nh, e) for buf in (k_buf, v_buf) for d in DILATIONS]
    return pl.pallas_call(
        _attn_sample_kernel,
        out_shape=_sds((nb, nh, e), BF16),
        grid=(nb,),
        in_specs=[tok, tok, tok, pl.BlockSpec((nh, 1), lambda b: (0, 0))] + [hbm] * len(views),
        out_specs=tok,
        scratch_shapes=[win, win, pltpu.SemaphoreType.DMA((2, 2, len(DILATIONS)))],
        compiler_params=_cparams(("arbitrary",)),
        name="attn_sample",
    )(q, k_new, v_new, slopes.reshape(nh, 1), *views)


def _shifted(u, prev8, shift):
    row = lax.broadcasted_iota(jnp.int32, u.shape, 0)
    out = pltpu.roll(u, shift, 0)
    for s in range(shift):
        out = jnp.where(row == s, prev8[8 - shift + s:8 - shift + s + 1, :], out)
    return out


def _sconv_prompt_kernel(h_ref, gb_ref, gc_ref, hp_ref, gcp_ref, w_ref, o_ref):
    i = pl.program_id(1)
    u = gc_ref[...] * h_ref[...]
    up = jnp.where(i == 0, 0.0, gcp_ref[...] * hp_ref[...])
    w = w_ref[...]
    conv = w[0:1] * _shifted(u, up, 2) + w[1:2] * _shifted(u, up, 1) + w[2:3] * u
    o_ref[...] = (gb_ref[...] * conv).astype(o_ref.dtype)


def _sconv_prompt(proj, w, n_batch, seq, tm):
    sc = w.shape[1]
    nt = seq // tm
    cb = 3 * ATT_W // sc
    cur = lambda c: pl.BlockSpec((tm, sc), lambda b, i: (b * nt + i, cb + c))
    prev = lambda c: pl.BlockSpec((8, sc), lambda b, i: (jnp.maximum((b * nt + i) * (tm // 8) - 1, 0), cb + c))
    return pl.pallas_call(
        _sconv_prompt_kernel,
        out_shape=_sds((n_batch * seq, sc), BF16),
        grid=(n_batch, nt),
        in_specs=[cur(0), cur(1), cur(2), prev(0), prev(2), pl.BlockSpec((SC_WIDTH, sc), lambda b, i: (0, 0))],
        out_specs=pl.BlockSpec((tm, sc), lambda b, i: (b * nt + i, 0)),
        compiler_params=_cparams(("parallel", "parallel")),
        name="sconv_prompt",
    )(proj, proj, proj, proj, proj, w)


def _sconv_sample_kernel(h_ref, gb_ref, gc_ref, p0_ref, p1_ref, w_ref, o_ref, u_ref):
    u = gc_ref[...] * h_ref[...]
    w = w_ref[...]
    conv = w[0:1] * p0_ref[...] + w[1:2] * p1_ref[...] + w[2:3] * u
    o_ref[...] = (gb_ref[...] * conv).astype(o_ref.dtype)
    u_ref[...] = u


def _sconv_sample(h, gb, gc, p0, p1, w):
    nb, sc = h.shape
    return pl.pallas_call(
        _sconv_sample_kernel,
        out_shape=(_sds((nb, sc), BF16), _sds((nb, sc), F32)),
        name="sconv_sample",
    )(h, gb, gc, p0, p1, w)


def _qk_normalise(y, cb, n_qk_blocks):
    scale = jnp.where(cb < n_qk_blocks // 2, DN_DIM ** -0.5, 1.0)
    segs = []
    for s in range(y.shape[1] // DN_DIM):
        seg = y[:, s * DN_DIM:(s + 1) * DN_DIM]
        ss = jnp.sum(seg * seg, axis=-1, keepdims=True)
        segs.append(seg * (lax.rsqrt(ss + L2_EPS) * scale))
    return jnp.concatenate(segs, axis=1)


def _dconv_prompt_kernel(x_ref, xp_ref, w_ref, o_ref, *, n_qk_blocks):
    cb = pl.program_id(0)
    i = pl.program_id(2)
    x = x_ref[...]
    xp = jnp.where(i == 0, 0.0, xp_ref[...])
    w = w_ref[...]
    conv = (w[0:1] * _shifted(x, xp, 3) + w[1:2] * _shifted(x, xp, 2)
            + w[2:3] * _shifted(x, xp, 1) + w[3:4] * x)
    y = _silu(conv)

    @pl.when(cb < n_qk_blocks)
    def _():
        o_ref[...] = _qk_normalise(y, cb, n_qk_blocks)

    @pl.when(cb >= n_qk_blocks)
    def _():
        o_ref[...] = y


def _dconv_prompt(proj, w, n_batch, seq, tm, tc):
    nt = seq // tm
    ncb = DN_CONV_DIM // tc
    return pl.pallas_call(
        functools.partial(_dconv_prompt_kernel, n_qk_blocks=2 * DN_QK_W // tc),
        out_shape=_sds((n_batch * seq, DN_CONV_DIM), F32),
        grid=(ncb, n_batch, nt),
        in_specs=[pl.BlockSpec((tm, tc), lambda c, b, i: (b * nt + i, c)),
                  pl.BlockSpec((8, tc), lambda c, b, i: (jnp.maximum((b * nt + i) * (tm // 8) - 1, 0), c)),
                  pl.BlockSpec((DN_CONV_WIDTH, tc), lambda c, b, i: (0, c))],
        out_specs=pl.BlockSpec((tm, tc), lambda c, b, i: (b * nt + i, c)),
        compiler_params=_cparams(("parallel", "parallel", "parallel")),
        name="dconv_prompt",
    )(proj, proj, w)


def _dconv_sample_kernel(x_ref, p0_ref, p1_ref, p2_ref, w_ref, o_ref, *, n_qk_blocks):
    cb = pl.program_id(0)
    w = w_ref[...]
    conv = w[0:1] * p0_ref[...] + w[1:2] * p1_ref[...] + w[2:3] * p2_ref[...] + w[3:4] * x_ref[...]
    y = _silu(conv)

    @pl.when(cb < n_qk_blocks)
    def _():
        o_ref[...] = _qk_normalise(y, cb, n_qk_blocks)

    @pl.when(cb >= n_qk_blocks)
    def _():
        o_ref[...] = y


def _dconv_sample(x, p0, p1, p2, w, tc):
    nb = x.shape[0]
    blk = pl.BlockSpec((nb, tc), lambda c: (0, c))
    return pl.pallas_call(
        functools.partial(_dconv_sample_kernel, n_qk_blocks=2 * DN_QK_W // tc),
        out_shape=_sds((nb, DN_CONV_DIM), F32),
        grid=(DN_CONV_DIM // tc,),
        in_specs=[blk, blk, blk, blk, pl.BlockSpec((DN_CONV_WIDTH, tc), lambda c: (0, c))],
        out_specs=blk,
        compiler_params=_cparams(("parallel",)),
        name="dconv_sample",
    )(x, p0, p1, p2, w)


def _gated_rms(o, z, norm_w):
    return o * lax.rsqrt(jnp.mean(o * o, axis=-1, keepdims=True) + RMS_EPS) * norm_w * _silu(z)


def _delta_prompt_kernel(q_ref, k_ref, v_ref, z_ref, g_ref, beta_ref, nw_ref, o_ref, s_out_ref, s_ref,
                         *, n_chunks):
    c = pl.program_id(2)

    @pl.when(c == 0)
    def _():
        s_ref[...] = jnp.zeros_like(s_ref)

    cc = DN_CHUNK
    ii = lax.broadcasted_iota(jnp.int32, (cc, cc), 0)
    jj = lax.broadcasted_iota(jnp.int32, (cc, cc), 1)
    lower = (ii >= jj).astype(BF16)
    upper = (ii <= jj).astype(BF16)
    ones = jnp.ones((cc, cc), BF16)
    g_all = g_ref[...]
    beta_all = beta_ref[...]
    norm_w = nw_ref[...]

    for hq in range(DN_GROUP_QK):
        q = q_ref[:, hq * DN_DIM:(hq + 1) * DN_DIM]
        k = k_ref[:, hq * DN_DIM:(hq + 1) * DN_DIM]
        k_b = k.astype(BF16)
        qk = _dot_nt(q.astype(BF16), k_b)
        for j in range(2):
            hv = 2 * hq + j
            g_col = g_all[:, hv:hv + 1]
            beta_col = beta_all[:, hv:hv + 1]
            gc = _dot_exact_lhs(lower, jnp.broadcast_to(g_col, (cc, DN_DIM)))
            gr = _dot_exact_lhs(ones, jnp.broadcast_to(g_col, (cc, cc)) * upper.astype(F32))
            decay = jnp.where(ii >= jj, jnp.exp(jnp.minimum(gc[:, :cc] - gr, 0.0)), 0.0)
            eg = jnp.exp(gc)
            kbeta = k * beta_col
            a_mat = jnp.where(ii > jj, _dot_nt(kbeta.astype(BF16), k_b) * decay, 0.0)
            y = jnp.concatenate([v_ref[:, hv * DN_DIM:(hv + 1) * DN_DIM] * beta_col, kbeta * eg], axis=1)
            npow = -a_mat
            for step in range(6):
                y = y + _dot3(npow, y)
                if step < 5:
                    npow = _dot3(npow, npow)
            u = y[:, :DN_DIM]
            w = y[:, DN_DIM:]
            s_old = s_ref[hv]
            s_b = s_old.astype(BF16)
            v_new = u - _dot(w.astype(BF16), s_b)
            v_new_b = v_new.astype(BF16)
            o = _dot((q * eg).astype(BF16), s_b) + _dot((qk * decay).astype(BF16), v_new_b)
            g_last = gc[cc - 1:cc, :]
            k_dec = k * jnp.exp(g_last - gc)
            s_ref[hv] = s_old * jnp.exp(g_last) + _dot_tn(k_dec.astype(BF16), v_new_b)
            z = z_ref[:, hv * DN_DIM:(hv + 1) * DN_DIM]
            o_ref[:, hv * DN_DIM:(hv + 1) * DN_DIM] = _gated_rms(o, z, norm_w).astype(o_ref.dtype)

    @pl.when(c == n_chunks - 1)
    def _():
        s_out_ref[...] = s_ref[...]


def _delta_prompt(qkv, proj, g, beta, norm_w, n_batch, seq):
    assert seq % DN_CHUNK == 0
    nc = seq // DN_CHUNK
    n_hg = DN_QK_HEADS // DN_GROUP_QK
    wq = DN_GROUP_QK * DN_DIM
    wv = DN_GROUP_V * DN_DIM
    row = lambda b, hg, c: b * nc + c
    return pl.pallas_call(
        functools.partial(_delta_prompt_kernel, n_chunks=nc),
        out_shape=(_sds((n_batch * seq, DN_VW), BF16), _sds((n_batch, DN_V_HEADS, DN_DIM, DN_DIM), F32)),
        grid=(n_batch, n_hg, nc),
        in_specs=[pl.BlockSpec((DN_CHUNK, wq), lambda b, hg, c: (row(b, hg, c), hg)),
                  pl.BlockSpec((DN_CHUNK, wq), lambda b, hg, c: (row(b, hg, c), n_hg + hg)),
                  pl.BlockSpec((DN_CHUNK, wv), lambda b, hg, c: (row(b, hg, c), 2 * DN_QK_W // wv + hg)),
                  pl.BlockSpec((DN_CHUNK, wv), lambda b, hg, c: (row(b, hg, c), DN_CONV_DIM // wv + hg)),
                  pl.BlockSpec((None, DN_CHUNK, DN_GROUP_V), lambda b, hg, c: (hg, row(b, hg, c), 0)),
                  pl.BlockSpec((None, DN_CHUNK, DN_GROUP_V), lambda b, hg, c: (hg, row(b, hg, c), 0)),
                  pl.BlockSpec((1, DN_DIM), lambda b, hg, c: (0, 0))],
        out_specs=(pl.BlockSpec((DN_CHUNK, wv), lambda b, hg, c: (row(b, hg, c), hg)),
                   pl.BlockSpec((None, DN_GROUP_V, DN_DIM, DN_DIM), lambda b, hg, c: (b, hg, 0, 0))),
        scratch_shapes=[pltpu.VMEM((DN_GROUP_V, DN_DIM, DN_DIM), F32)],
        compiler_params=_cparams(("parallel", "parallel", "arbitrary")),
        name="delta_prompt",
    )(qkv, qkv, qkv, proj, g, beta, norm_w.reshape(1, DN_DIM))


def _delta_sample_kernel(qt_ref, kt_ref, v_ref, z_ref, g_ref, beta_ref, nw_ref, s_ref, o_ref, s_out_ref):
    norm_w = nw_ref[...]
    qt = qt_ref[...]
    kt = kt_ref[...]
    for hv in range(DN_V_HEADS):
        hq = hv // 2
        k_col = jnp.broadcast_to(kt[:, hq:hq + 1], (DN_DIM, DN_DIM))
        q_col = jnp.broadcast_to(qt[:, hq:hq + 1], (DN_DIM, DN_DIM))
        eg = jnp.exp(g_ref[hv:hv + 1, :])
        beta = beta_ref[hv:hv + 1, :]
        s = s_ref[hv]
        sk = jnp.sum(s * k_col, axis=0, keepdims=True)
        delta = beta * (v_ref[hv:hv + 1, :] - eg * sk)
        s_new = s * eg + k_col * delta
        s_out_ref[hv] = s_new
        o = jnp.sum(s_new * q_col, axis=0, keepdims=True)
        o_ref[hv:hv + 1, :] = _gated_rms(o, z_ref[hv:hv + 1, :], norm_w).astype(o_ref.dtype)


def _delta_sample(qt, kt, v, z, g_b, beta_b, norm_w, s0):
    nb = v.shape[0]
    per_b3 = lambda shape: pl.BlockSpec((None,) + shape, lambda b: (b, 0, 0))
    st = pl.BlockSpec((None, DN_V_HEADS, DN_DIM, DN_DIM), lambda b: (b, 0, 0, 0))
    return pl.pallas_call(
        _delta_sample_kernel,
        out_shape=(_sds((nb, DN_V_HEADS, DN_DIM), BF16), _sds(s0.shape, F32)),
        grid=(nb,),
        in_specs=[per_b3((DN_DIM, DN_QK_HEADS)), per_b3((DN_DIM, DN_QK_HEADS)),
                  per_b3((DN_V_HEADS, DN_DIM)), per_b3((DN_V_HEADS, DN_DIM)),
                  per_b3((DN_V_HEADS, DN_DIM)), per_b3((DN_V_HEADS, DN_DIM)),
                  pl.BlockSpec((1, DN_DIM), lambda b: (0, 0)), st],
        out_specs=(per_b3((DN_V_HEADS, DN_DIM)), st),
        compiler_params=_cparams(("parallel",)),
        name="delta_sample",
    )(qt, kt, v, z, g_b, beta_b, norm_w.reshape(1, DN_DIM), s0)


def _mixer_layer(xf, xb, w_in, sconv_w, w_out, k_buf, v_buf, sconv_prev, n_batch, seq, g1, b1):
    tp = n_batch * seq
    proj = _matmul(xb, w_in.astype(BF16), 832, 512)
    slopes = jnp.exp2(-8.0 * (jnp.arange(ATT_HEADS, dtype=F32) + 1.0) / ATT_HEADS)
    att_p = _attn_prompt(proj, slopes, n_batch, seq)
    ns = xf.shape[0] - tp
    ps = proj[tp:]
    q_s, k_s, v_s = (ps[:, i * ATT_W:(i + 1) * ATT_W] for i in range(3))
    heads = lambda a: a.reshape(ns, ATT_HEADS, HEAD_DIM)
    att_s = _attn_sample(heads(q_s), heads(k_s), heads(v_s), k_buf, v_buf, slopes)
    sc = sconv_w.shape[1]
    gc_p = _sconv_prompt(proj, sconv_w, n_batch, seq, 512)
    h_s, gb_s, gcs_s = (ps[:, 3 * ATT_W + i * sc:3 * ATT_W + (i + 1) * sc] for i in range(3))
    gc_s, u_s = _sconv_sample(h_s, gb_s, gcs_s, sconv_prev[:, 0], sconv_prev[:, 1], sconv_w)
    mixed = jnp.concatenate([jnp.concatenate([att_p, att_s.reshape(ns, ATT_W)], axis=0),
                             jnp.concatenate([gc_p, gc_s], axis=0)], axis=1)
    x1f, x1b = _matmul_residual_ln(mixed, w_out.astype(BF16), xf, g1, b1, 320, w_out.shape[0])

    keep = min(SLAB, seq)
    pp = proj[:tp].reshape(n_batch, seq, -1)
    k_rows_p = pp[:, seq - keep:, ATT_W:2 * ATT_W].reshape(n_batch, keep, ATT_HEADS, HEAD_DIM)
    v_rows_p = pp[:, seq - keep:, 2 * ATT_W:3 * ATT_W].reshape(n_batch, keep, ATT_HEADS, HEAD_DIM)
    u_tail = pp[:, seq - (SC_WIDTH - 1):, 3 * ATT_W + 2 * sc:] * pp[:, seq - (SC_WIDTH - 1):, 3 * ATT_W:3 * ATT_W + sc]
    sconv_new_s = jnp.stack([sconv_prev[:, 1], u_s], axis=1)
    outs_p = (k_rows_p, v_rows_p, u_tail)
    outs_s = (k_s.reshape(ns, 1, ATT_HEADS, HEAD_DIM), v_s.reshape(ns, 1, ATT_HEADS, HEAD_DIM), sconv_new_s)
    return x1f, x1b, outs_p, outs_s


def _delta_layer(xf, xb, w_in, conv_w, a_log, dt_bias, norm_w, w_out, conv_prev, s0, n_batch, seq, g1, b1):
    tp = n_batch * seq
    ns = xf.shape[0] - tp
    n_main = DN_CONV_DIM + DN_VW
    proj = _matmul(xb, w_in[:, :n_main].astype(BF16), 832, 512)
    ba = _matmul(xb, w_in[:, n_main:].astype(BF16), 832, 2 * DN_V_HEADS)
    beta = jax.nn.sigmoid(ba[:, :DN_V_HEADS])
    g = -jnp.exp(a_log.astype(F32)) * jax.nn.softplus(ba[:, DN_V_HEADS:] + dt_bias.astype(F32))

    qkv_p = _dconv_prompt(proj, conv_w, n_batch, seq, 512, 512)
    n_vg = DN_V_HEADS // DN_GROUP_V
    grp = lambda a: a[:tp].reshape(tp, n_vg, DN_GROUP_V).transpose(1, 0, 2)
    o_p, s_p = _delta_prompt(qkv_p, proj, grp(g), grp(beta), norm_w, n_batch, seq)

    ps = proj[tp:]
    qkv_s = _dconv_sample(ps[:, :DN_CONV_DIM], conv_prev[:, 0], conv_prev[:, 1], conv_prev[:, 2], conv_w, 512)
    qt = qkv_s[:, :DN_QK_W].reshape(ns, DN_QK_HEADS, DN_DIM).transpose(0, 2, 1)
    kt = qkv_s[:, DN_QK_W:2 * DN_QK_W].reshape(ns, DN_QK_HEADS, DN_DIM).transpose(0, 2, 1)
    v_s = qkv_s[:, 2 * DN_QK_W:].reshape(ns, DN_V_HEADS, DN_DIM)
    z_s = ps[:, DN_CONV_DIM:].reshape(ns, DN_V_HEADS, DN_DIM)
    lanes = lambda a: jnp.broadcast_to(a[tp:, :, None], (ns, DN_V_HEADS, DN_DIM))
    o_s, s_s = _delta_sample(qt, kt, v_s, z_s, lanes(g), lanes(beta), norm_w, s0)

    og = jnp.concatenate([o_p, o_s.reshape(ns, DN_VW)], axis=0)
    x1f, x1b = _matmul_residual_ln(og, w_out.astype(BF16), xf, g1, b1, 320, 2048)

    dconv_new_p = proj[:tp].reshape(n_batch, seq, -1)[:, seq - (DN_CONV_WIDTH - 1):, :DN_CONV_DIM]
    dconv_new_s = jnp.concatenate([conv_prev[:, 1:], ps[:, None, :DN_CONV_DIM]], axis=1)
    return x1f, x1b, (dconv_new_p, s_p), (dconv_new_s, s_s)


def _moe_block(x1f, x1b, p_b, w_router, e_bias, w_gate, w_up, w_down, ws_gate, ws_up, ws_down,
               ple_w_proj, ple_w_gate, g2, b2):
    t, d = x1f.shape
    logits = _matmul(x1b, w_router.astype(BF16), 832, N_EXPERTS)
    e_idx, wts = _route(logits, e_bias)
    row_tok, tile_expert, tile_valid, dest_of_assign = _dispatch_plan(e_idx)
    y_rows = _moe_experts(x1b[row_tok], tile_expert, tile_valid, w_gate, w_up, w_down)
    y_assign = y_rows[dest_of_assign].reshape(t, TOP_K, d)
    routed = jnp.einsum('tk,tkd->td', wts.astype(BF16).astype(F32), y_assign.astype(F32))
    return _final_block(x1f, x1b, routed, p_b, ws_gate.astype(BF16), ws_up.astype(BF16),
                        ws_down.astype(BF16), ple_w_gate.astype(BF16), ple_w_proj.astype(BF16), g2, b2, 320)


def kernel(x_prompt, x_sample, cache_attn_k, cache_attn_v, state_sconv, state_dconv, state_delta,
           p_prompt, p_sample, mix_w_in, mix_sconv_w, mix_w_out, dn_w_in, dn_conv_w, dn_a_log,
           dn_dt_bias, dn_norm_w, dn_w_out, ln1_g, ln1_b, ln2_g, ln2_b, moe_w_router, moe_e_bias,
           moe_w_gate, moe_w_up, moe_w_down, moe_ws_gate, moe_ws_up, moe_ws_down, ple_w_proj,
           ple_w_gate):
    n_batch, seq, d = x_prompt.shape
    ns = x_sample.shape[0]
    assert x_sample.shape[1] == 1
    tp = n_batch * seq
    xf = jnp.concatenate([x_prompt.reshape(tp, d), x_sample.reshape(ns, d)], axis=0)
    xb = xf.astype(BF16)
    p_all = jnp.concatenate([p_prompt.reshape(DEPTH, tp, -1), p_sample.reshape(DEPTH, ns, -1)],
                            axis=1).astype(BF16)

    mix_p, mix_s, dn_p, dn_s = [], [], [], []
    for i in range(DEPTH):
        j = i // 2
        if i % 2 == 0:
            x1f, x1b, o_p, o_s = _mixer_layer(
                xf, xb, mix_w_in[j], mix_sconv_w[j], mix_w_out[j], cache_attn_k[j], cache_attn_v[j],
                state_sconv[j], n_batch, seq, ln1_g[i], ln1_b[i])
            mix_p.append(o_p)
            mix_s.append(o_s)
        else:
            x1f, x1b, o_p, o_s = _delta_layer(
                xf, xb, dn_w_in[j], dn_conv_w[j], dn_a_log[j], dn_dt_bias[j], dn_norm_w[j], dn_w_out[j],
                state_dconv[j], state_delta[j], n_batch, seq, ln1_g[i], ln1_b[i])
            dn_p.append(o_p)
            dn_s.append(o_s)
        xf, xb = _moe_block(x1f, x1b, p_all[i], moe_w_router[i], moe_e_bias[i], moe_w_gate[i],
                            moe_w_up[i], moe_w_down[i], moe_ws_gate[i], moe_ws_up[i], moe_ws_down[i],
                            ple_w_proj[i], ple_w_gate[i], ln2_g[i], ln2_b[i])

    stack = lambda parts, idx: jnp.stack([p[idx] for p in parts])
    y_prompt = xf[:tp].reshape(n_batch, seq, d)
    y_sample = xf[tp:].reshape(ns, 1, d)
    return (y_prompt, y_sample,
            stack(mix_p, 0), stack(mix_p, 1), stack(mix_p, 2), stack(dn_p, 0), stack(dn_p, 1),
            stack(mix_s, 0), stack(mix_s, 1), stack(mix_s, 2), stack(dn_s, 0), stack(dn_s, 1))
```

```python
import functools
import math

import jax
import jax.numpy as jnp
from jax import lax
from jax.experimental import pallas as pl
from jax.experimental.pallas import tpu as pltpu

F32 = jnp.float32
BF16 = jnp.bfloat16

DEPTH = 2
HEAD_DIM = 128
ATT_HEADS = 12
ATT_W = ATT_HEADS * HEAD_DIM
SC_WIDTH = 3
DILATIONS = (1, 4, 16)
BAND = 128
SLAB = BAND * DILATIONS[-1]
ATT_SCALE = HEAD_DIM ** -0.5
DN_QK_HEADS = 16
DN_V_HEADS = 32
DN_DIM = 128
DN_QK_W = DN_QK_HEADS * DN_DIM
DN_VW = DN_V_HEADS * DN_DIM
DN_CONV_DIM = 2 * DN_QK_W + DN_VW
DN_CONV_WIDTH = 4
DN_CHUNK = 64
DN_GROUP_QK = 4
DN_GROUP_V = 2 * DN_GROUP_QK
N_EXPERTS = 64
N_GROUPS = 8
TOPK_GROUPS = 4
TOP_K = 8
ROUTED_SCALE = 2.5
ALPHA = (2 * DEPTH) ** 0.25
LN_EPS = 1e-5
RMS_EPS = 1e-6
L2_EPS = 1e-6
NEG = -1e30

MOE_TM = 256
VMEM_LIMIT = 56 * 1024 * 1024


def _sds(shape, dtype):
    return jax.ShapeDtypeStruct(shape, dtype)


def _cparams(sem):
    return pltpu.CompilerParams(dimension_semantics=sem, vmem_limit_bytes=VMEM_LIMIT)


def _silu(x):
    return x * jax.nn.sigmoid(x)


def _layer_norm_rows(v, g, b):
    mu = jnp.mean(v, axis=-1, keepdims=True)
    c = v - mu
    var = jnp.mean(c * c, axis=-1, keepdims=True)
    return c * lax.rsqrt(var + LN_EPS) * g + b


def _split3(x):
    hi = x.astype(BF16)
    r1 = x - hi.astype(F32)
    mid = r1.astype(BF16)
    lo = (r1 - mid.astype(F32)).astype(BF16)
    return hi, mid, lo


def _dot(a, b):
    return jnp.dot(a, b, preferred_element_type=F32)


def _dot_nt(a, b):
    return lax.dot_general(a, b, (((1,), (1,)), ((), ())), preferred_element_type=F32)


def _dot_tn(a, b):
    return lax.dot_general(a, b, (((0,), (0,)), ((), ())), preferred_element_type=F32)


def _dot_exact_lhs(sel_bf16, x):
    hi, mid, lo = _split3(x)
    return _dot(sel_bf16, hi) + _dot(sel_bf16, mid) + _dot(sel_bf16, lo)


def _dot_exact_rhs(x, sel_bf16):
    hi, mid, lo = _split3(x)
    return _dot(hi, sel_bf16) + _dot(mid, sel_bf16) + _dot(lo, sel_bf16)


def _dot3(a, b):
    a_hi = a.astype(BF16)
    a_lo = (a - a_hi.astype(F32)).astype(BF16)
    b_hi = b.astype(BF16)
    b_lo = (b - b_hi.astype(F32)).astype(BF16)
    return _dot(a_hi, b_hi) + _dot(a_hi, b_lo) + _dot(a_lo, b_hi)


def _mm_kernel(x_ref, w_ref, o_ref):
    o_ref[...] = _dot(x_ref[...], w_ref[...])


def _matmul(x, w, tm, tn):
    m, k = x.shape
    n = w.shape[1]
    assert m % tm == 0 and n % tn == 0
    return pl.pallas_call(
        _mm_kernel,
        out_shape=_sds((m, n), F32),
        grid=(m // tm, n // tn),
        in_specs=[pl.BlockSpec((tm, k), lambda i, j: (i, 0)),
                  pl.BlockSpec((k, tn), lambda i, j: (0, j))],
        out_specs=pl.BlockSpec((tm, tn), lambda i, j: (i, j)),
        compiler_params=_cparams(("parallel", "parallel")),
        name="matmul",
    )(x, w)


def _mm_ln_kernel(a_ref, w_ref, x_ref, g_ref, b_ref, of_ref, ob_ref, acc_ref, *, nk):
    kk = pl.program_id(1)
    part = _dot(a_ref[...], w_ref[...])

    @pl.when(kk == 0)
    def _():
        acc_ref[...] = part

    @pl.when(kk > 0)
    def _():
        acc_ref[...] += part

    @pl.when(kk == nk - 1)
    def _():
        y = _layer_norm_rows(ALPHA * x_ref[...] + acc_ref[...], g_ref[...], b_ref[...])
        of_ref[...] = y
        ob_ref[...] = y.astype(BF16)


def _matmul_residual_ln(a, w, x, g, b, tm, tk):
    m, k = a.shape
    d = w.shape[1]
    assert m % tm == 0 and k % tk == 0
    nk = k // tk
    return pl.pallas_call(
        functools.partial(_mm_ln_kernel, nk=nk),
        out_shape=(_sds((m, d), F32), _sds((m, d), BF16)),
        grid=(m // tm, nk),
        in_specs=[pl.BlockSpec((tm, tk), lambda i, kk: (i, kk)),
                  pl.BlockSpec((tk, d), lambda i, kk: (kk, 0)),
                  pl.BlockSpec((tm, d), lambda i, kk: (i, 0)),
                  pl.BlockSpec((1, d), lambda i, kk: (0, 0)),
                  pl.BlockSpec((1, d), lambda i, kk: (0, 0))],
        out_specs=(pl.BlockSpec((tm, d), lambda i, kk: (i, 0)),
                   pl.BlockSpec((tm, d), lambda i, kk: (i, 0))),
        scratch_shapes=[pltpu.VMEM((tm, d), F32)],
        compiler_params=_cparams(("parallel", "arbitrary")),
        name="matmul_residual_ln",
    )(a, w, x, g.reshape(1, d), b.reshape(1, d))


def _moe_kernel(te_ref, tv_ref, x_ref, wg_ref, wu_ref, wd_ref, o_ref, wg_s, wu_s, wd_s):
    i = pl.program_id(0)
    e = te_ref[i]
    prev = te_ref[jnp.maximum(i - 1, 0)]

    @pl.when(jnp.logical_or(i == 0, e != prev))
    def _():
        wg_s[...] = wg_ref[...].astype(BF16)
        wu_s[...] = wu_ref[...].astype(BF16)
        wd_s[...] = wd_ref[...].astype(BF16)

    @pl.when(tv_ref[i] > 0)
    def _():
        x = x_ref[...]
        a = _dot(x, wg_s[...])
        u = _dot(x, wu_s[...])
        o_ref[...] = _dot((_silu(a) * u).astype(BF16), wd_s[...]).astype(o_ref.dtype)

    @pl.when(tv_ref[i] == 0)
    def _():
        o_ref[...] = jnp.zeros_like(o_ref)


def _moe_experts(x_rows, tile_expert, tile_valid, w_gate, w_up, w_down):
    n_rows, d = x_rows.shape
    n_tiles = n_rows // MOE_TM
    de = w_gate.shape[-1]
    grid_spec = pltpu.PrefetchScalarGridSpec(
        num_scalar_prefetch=2,
        grid=(n_tiles,),
        in_specs=[pl.BlockSpec((MOE_TM, d), lambda i, te, tv: (i, 0)),
                  pl.BlockSpec((None, d, de), lambda i, te, tv: (te[i], 0, 0)),
                  pl.BlockSpec((None, d, de), lambda i, te, tv: (te[i], 0, 0)),
                  pl.BlockSpec((None, de, d), lambda i, te, tv: (te[i], 0, 0))],
        out_specs=pl.BlockSpec((MOE_TM, d), lambda i, te, tv: (i, 0)),
        scratch_shapes=[pltpu.VMEM((d, de), BF16), pltpu.VMEM((d, de), BF16),
                        pltpu.VMEM((de, d), BF16)],
    )
    return pl.pallas_call(
        _moe_kernel,
        out_shape=_sds((n_rows, d), BF16),
        grid_spec=grid_spec,
        compiler_params=_cparams(("arbitrary",)),
        name="moe_experts",
    )(tile_expert, tile_valid, x_rows, w_gate, w_up, w_down)


def _top_k_desc(vals, k):
    n = vals.shape[-1]
    iota = lax.broadcasted_iota(jnp.int32, vals.shape, vals.ndim - 1)
    top_v, top_i = [], []
    for _ in range(k):
        m = jnp.max(vals, axis=-1, keepdims=True)
        idx = jnp.min(jnp.where(vals == m, iota, n), axis=-1, keepdims=True)
        top_v.append(m)
        top_i.append(idx)
        vals = jnp.where(iota == idx, -jnp.inf, vals)
    return jnp.concatenate(top_v, axis=-1), jnp.concatenate(top_i, axis=-1)


def _route(logits, e_bias):
    t = logits.shape[0]
    scores = jax.nn.sigmoid(logits)
    sel = scores + e_bias.astype(F32)
    per_group = sel.reshape(t, N_GROUPS, N_EXPERTS // N_GROUPS)
    group_score = _top_k_desc(per_group, 2)[0].sum(-1)
    _, g_idx = _top_k_desc(group_score, TOPK_GROUPS)
    g_mask = (g_idx[:, :, None] == jnp.arange(N_GROUPS)[None, None, :]).any(axis=1)
    e_mask = jnp.repeat(g_mask, N_EXPERTS // N_GROUPS, axis=1)
    _, e_idx = _top_k_desc(jnp.where(e_mask, sel, -jnp.inf), TOP_K)
    wts = jnp.take_along_axis(scores, e_idx, axis=1)
    wts = wts / wts.sum(-1, keepdims=True) * ROUTED_SCALE
    return e_idx, wts


def _running_count_kernel(m_ref, o_ref, carry_ref):
    i = pl.program_id(0)

    @pl.when(i == 0)
    def _():
        carry_ref[...] = jnp.zeros_like(carry_ref)

    tm = m_ref.shape[0]
    lower = (lax.broadcasted_iota(jnp.int32, (tm, tm), 0)
             >= lax.broadcasted_iota(jnp.int32, (tm, tm), 1)).astype(BF16)
    counts = _dot(lower, m_ref[...]) + carry_ref[...]
    o_ref[...] = counts
    carry_ref[...] = counts[tm - 1:tm, :]


def _running_count(member, tm):
    t, n = member.shape
    assert t % tm == 0
    return pl.pallas_call(
        _running_count_kernel,
        out_shape=_sds((t, n), F32),
        grid=(t // tm,),
        in_specs=[pl.BlockSpec((tm, n), lambda i: (i, 0))],
        out_specs=pl.BlockSpec((tm, n), lambda i: (i, 0)),
        scratch_shapes=[pltpu.VMEM((1, n), F32)],
        compiler_params=_cparams(("arbitrary",)),
        name="running_count",
    )(member)


def _dispatch_plan(e_idx):
    t = e_idx.shape[0]
    n_assign = t * TOP_K
    member = (e_idx[:, :, None] == jnp.arange(N_EXPERTS)[None, None, :]).any(axis=1)
    running = _running_count(member.astype(BF16), math.gcd(t, 832)).astype(jnp.int32)
    counts = running[-1]
    rank = jnp.take_along_axis(running, e_idx, axis=1) - 1
    padded = (counts + MOE_TM - 1) // MOE_TM * MOE_TM
    pad_end = jnp.cumsum(padded)
    pad_start = pad_end - padded
    dest_of_assign = (pad_start[e_idx] + rank).astype(jnp.int32).reshape(-1)
    n_tiles = -(-(n_assign + N_EXPERTS * (MOE_TM - 1)) // MOE_TM)
    tok_of_assign = jnp.repeat(jnp.arange(t, dtype=jnp.int32), TOP_K)
    row_tok = jnp.zeros((n_tiles * MOE_TM,), jnp.int32).at[dest_of_assign].set(tok_of_assign)
    tile_start = jnp.arange(n_tiles) * MOE_TM
    tile_expert = jnp.minimum((tile_start[:, None] >= pad_end[None, :]).sum(axis=1),
                              N_EXPERTS - 1).astype(jnp.int32)
    tile_valid = (tile_start < pad_end[-1]).astype(jnp.int32)
    last_used = tile_expert[jnp.maximum(pad_end[-1] // MOE_TM - 1, 0)]
    tile_expert = jnp.where(tile_valid > 0, tile_expert, last_used)
    return row_tok, tile_expert, tile_valid, dest_of_assign


def _final_kernel(xf_ref, xb_ref, y_ref, wt_ref, p_ref, wsg_ref, wsu_ref, wsd_ref, wpg_ref, wpp_ref,
                  g_ref, b_ref, of_ref, ob_ref):
    d = xf_ref.shape[1]
    wt = wt_ref[...].astype(BF16).astype(F32)
    routed = wt[:, 0:1] * y_ref[:, 0:d].astype(F32)
    for j in range(1, wt.shape[1]):
        routed = routed + wt[:, j:j + 1] * y_ref[:, j * d:(j + 1) * d].astype(F32)
    x = xb_ref[...]
    a = _dot(x, wsg_ref[...])
    u = _dot(x, wsu_ref[...])
    shared = _dot((_silu(a) * u).astype(BF16), wsd_ref[...])
    gate = jax.nn.sigmoid(_dot(x, wpg_ref[...]))
    ple = gate * _dot(p_ref[...], wpp_ref[...])
    v = ALPHA * xf_ref[...] + (routed + shared) + ple
    y = _layer_norm_rows(v, g_ref[...], b_ref[...])
    of_ref[...] = y
    ob_ref[...] = y.astype(BF16)


def _final_block(xf, xb, y_assign, wts, p, wsg, wsu, wsd, wpg, wpp, g, b, tm):
    m, d = xf.shape
    de = wsg.shape[1]
    dp = p.shape[1]
    nk = wts.shape[1]
    assert m % tm == 0
    row = lambda i: (i, 0)
    const = lambda i: (0, 0)
    once = dict(pipeline_mode=pl.Buffered(1))
    return pl.pallas_call(
        _final_kernel,
        out_shape=(_sds((m, d), F32), _sds((m, d), BF16)),
        grid=(m // tm,),
        in_specs=[pl.BlockSpec((tm, d), row), pl.BlockSpec((tm, d), row),
                  pl.BlockSpec((tm, nk * d), row), pl.BlockSpec((tm, nk), row), pl.BlockSpec((tm, dp), row),
                  pl.BlockSpec((d, de), const, **once), pl.BlockSpec((d, de), const, **once),
                  pl.BlockSpec((de, d), const, **once), pl.BlockSpec((d, d), const, **once),
                  pl.BlockSpec((dp, d), const, **once),
                  pl.BlockSpec((1, d), const), pl.BlockSpec((1, d), const)],
        out_specs=(pl.BlockSpec((tm, d), row), pl.BlockSpec((tm, d), row)),
        compiler_params=_cparams(("parallel",)),
        name="shared_ple_ln",
    )(xf, xb, y_assign, wts, p, wsg, wsu, wsd, wpg, wpp, g.reshape(1, d), b.reshape(1, d))


def _attn_prompt_kernel(slopes_ref, q_ref, k_ref, v_ref, o_ref, kk, vv, m_s, l_s, acc_s):
    h = pl.program_id(1)
    sl = pl.program_id(2)

    @pl.when(sl == 0)
    def _():
        kk[0:SLAB, :] = jnp.zeros((SLAB, HEAD_DIM), F32)
        vv[0:SLAB, :] = jnp.zeros((SLAB, HEAD_DIM), F32)

    kk[SLAB:2 * SLAB, :] = k_ref[...]
    vv[SLAB:2 * SLAB, :] = v_ref[...]

    slope = slopes_ref[h]
    ii = lax.broadcasted_iota(jnp.int32, (BAND, BAND), 0)
    jj = lax.broadcasted_iota(jnp.int32, (BAND, BAND), 1)
    dist_prev = (ii + BAND - jj).astype(F32)
    dist_cur = (ii - jj).astype(F32)

    for di, d in enumerate(DILATIONS):
        c = slope * float(d)
        bias_prev = jnp.where(jj >= ii, -c * dist_prev, NEG)
        bias_cur = jnp.where(jj <= ii, -c * dist_cur, NEG)

        def unit(u, carry, di=di, d=d, bias_prev=bias_prev, bias_cur=bias_cur):
            sp = u // d
            r = u - sp * d
            base = sp * (BAND * d) + r

            def rows(start):
                if d == 1:
                    return pl.ds(start, BAND)
                return pl.ds(start, BAND, stride=d)

            q = q_ref[rows(base), :].astype(BF16)
            kc = kk[rows(base + SLAB), :].astype(BF16)
            vc = vv[rows(base + SLAB), :].astype(BF16)
            kp = kk[rows(base + SLAB - BAND * d), :].astype(BF16)
            vp = vv[rows(base + SLAB - BAND * d), :].astype(BF16)
            pen = jnp.where(jnp.logical_and(sl == 0, sp == 0), NEG, 0.0)
            s_p = _dot_nt(q, kp) * ATT_SCALE + (bias_prev + pen)
            s_c = _dot_nt(q, kc) * ATT_SCALE + bias_cur
            m = jnp.maximum(jnp.max(s_p, axis=-1, keepdims=True), jnp.max(s_c, axis=-1, keepdims=True))
            p_p = jnp.exp(s_p - m)
            p_c = jnp.exp(s_c - m)
            l = jnp.sum(p_p, axis=-1, keepdims=True) + jnp.sum(p_c, axis=-1, keepdims=True)
            m_s[di, rows(base), :] = jnp.broadcast_to(m, (BAND, HEAD_DIM))
            l_s[di, rows(base), :] = jnp.broadcast_to(l, (BAND, HEAD_DIM))
            acc_s[di, rows(base), :] = _dot(p_p.astype(BF16), vp) + _dot(p_c.astype(BF16), vc)
            return carry

        lax.fori_loop(0, SLAB // BAND, unit, 0)

    def merge(t, carry):
        rows = pl.ds(pl.multiple_of(t * BAND, BAND), BAND)
        m_max = jnp.maximum(jnp.maximum(m_s[0, rows, :], m_s[1, rows, :]), m_s[2, rows, :])
        w0 = jnp.exp(m_s[0, rows, :] - m_max)
        num = w0 * acc_s[0, rows, :]
        den = w0 * l_s[0, rows, :]
        for di in range(1, len(DILATIONS)):
            w = jnp.exp(m_s[di, rows, :] - m_max)
            num = num + w * acc_s[di, rows, :]
            den = den + w * l_s[di, rows, :]
        o_ref[rows, :] = (num / den).astype(o_ref.dtype)
        return carry

    lax.fori_loop(0, SLAB // BAND, merge, 0)
    kk[0:SLAB, :] = kk[SLAB:2 * SLAB, :]
    vv[0:SLAB, :] = vv[SLAB:2 * SLAB, :]


def _attn_prompt(proj, slopes, n_batch, seq):
    assert seq % SLAB == 0
    ns = seq // SLAB
    grid_spec = pltpu.PrefetchScalarGridSpec(
        num_scalar_prefetch=1,
        grid=(n_batch, ATT_HEADS, ns),
        in_specs=[pl.BlockSpec((SLAB, HEAD_DIM), lambda b, h, s, sl: (b * ns + s, h)),
                  pl.BlockSpec((SLAB, HEAD_DIM), lambda b, h, s, sl: (b * ns + s, ATT_HEADS + h)),
                  pl.BlockSpec((SLAB, HEAD_DIM), lambda b, h, s, sl: (b * ns + s, 2 * ATT_HEADS + h))],
        out_specs=pl.BlockSpec((SLAB, HEAD_DIM), lambda b, h, s, sl: (b * ns + s, h)),
        scratch_shapes=[pltpu.VMEM((2 * SLAB, HEAD_DIM), F32), pltpu.VMEM((2 * SLAB, HEAD_DIM), F32),
                        pltpu.VMEM((len(DILATIONS), SLAB, HEAD_DIM), F32),
                        pltpu.VMEM((len(DILATIONS), SLAB, HEAD_DIM), F32),
                        pltpu.VMEM((len(DILATIONS), SLAB, HEAD_DIM), F32)],
    )
    return pl.pallas_call(
        _attn_prompt_kernel,
        out_shape=_sds((n_batch * seq, ATT_W), BF16),
        grid_spec=grid_spec,
        compiler_params=_cparams(("parallel", "parallel", "arbitrary")),
        name="attn_prompt",
    )(slopes, proj, proj, proj)


def _window_copy(view_hbm, dst, sems, b, slot, di, which):
    groups = view_hbm.shape[1]
    return pltpu.make_async_copy(view_hbm.at[b, pl.ds(groups - BAND, BAND), 0], dst.at[slot, di],
                                 sems.at[slot, which, di])


def _attn_sample_kernel(q_ref, kn_ref, vn_ref, slopes_ref, k1_hbm, k4_hbm, k16_hbm, v1_hbm, v4_hbm, v16_hbm,
                        o_ref, k_win, v_win, sems):
    b = pl.program_id(0)
    nb = pl.num_programs(0)
    slot = lax.rem(b, 2)
    k_views = (k1_hbm, k4_hbm, k16_hbm)
    v_views = (v1_hbm, v4_hbm, v16_hbm)

    def fetch(bb, sl, start):
        for di in range(len(DILATIONS)):
            for which, (src, dst) in enumerate(((k_views[di], k_win), (v_views[di], v_win))):
                cp = _window_copy(src, dst, sems, bb, sl, di, which)
                if start:
                    cp.start()
                else:
                    cp.wait()

    @pl.when(b == 0)
    def _():
        fetch(0, 0, True)

    @pl.when(b + 1 < nb)
    def _():
        fetch(b + 1, 1 - slot, True)

    fetch(b, slot, False)

    q = q_ref[...]
    slopes = slopes_ref[...]
    s_self = jnp.sum(q * kn_ref[...], axis=-1, keepdims=True) * ATT_SCALE
    steps = (BAND - lax.broadcasted_iota(jnp.int32, (BAND, 1, 1), 0)).astype(F32)
    parts = []
    for di, d in enumerate(DILATIONS):
        sc = (jnp.sum(k_win[slot, di] * q, axis=-1, keepdims=True) * ATT_SCALE
              - slopes * (steps * float(d)))
        m_d = jnp.maximum(jnp.max(sc, axis=0), s_self)
        p = jnp.exp(sc - m_d)
        p_self = jnp.exp(s_self - m_d)
        l_d = jnp.sum(p, axis=0) + p_self
        acc = jnp.sum(p * v_win[slot, di], axis=0) + p_self * vn_ref[...]
        parts.append((m_d, l_d, acc))
    m_max = jnp.maximum(jnp.maximum(parts[0][0], parts[1][0]), parts[2][0])
    num = jnp.zeros(q.shape, F32)
    den = jnp.zeros(s_self.shape, F32)
    for m_d, l_d, acc in parts:
        w = jnp.exp(m_d - m_max)
        num = num + w * acc
        den = den + w * l_d
    o_ref[...] = (num / den).astype(o_ref.dtype)


def _attn_sample(q, k_new, v_new, k_buf, v_buf, slopes):
    nb, n_buf, nh, e = k_buf.shape
    assert n_buf >= BAND * DILATIONS[-1] and all(n_buf % d == 0 for d in DILATIONS)
    tok = pl.BlockSpec((None, nh, e), lambda b: (b, 0, 0))
    hbm = pl.BlockSpec(memory_space=pl.ANY)
    win = pltpu.VMEM((2, len(DILATIONS), BAND, nh, e), F32)
    views = [buf.reshape(nb, n_buf // d, d, nh, e) for buf in (k_buf, v_buf) for d in DILATIONS]
    return pl.pallas_call(
        _attn_sample_kernel,
        out_shape=_sds((nb, nh, e), BF16),
        grid=(nb,),
        in_specs=[tok, tok, tok, pl.BlockSpec((nh, 1), lambda b: (0, 0))] + [hbm] * len(views),
        out_specs=tok,
        scratch_shapes=[win, win, pltpu.SemaphoreType.DMA((2, 2, len(DILATIONS)))],
        compiler_params=_cparams(("arbitrary",)),
        name="attn_sample",
    )(q, k_new, v_new, slopes.reshape(nh, 1), *views)


def _shifted(u, prev8, shift):
    row = lax.broadcasted_iota(jnp.int32, u.shape, 0)
    out = pltpu.roll(u, shift, 0)
    for s in range(shift):
        out = jnp.where(row == s, prev8[8 - shift + s:8 - shift + s + 1, :], out)
    return out


def _sconv_prompt_kernel(h_ref, gb_ref, gc_ref, hp_ref, gcp_ref, w_ref, o_ref):
    i = pl.program_id(1)
    u = gc_ref[...] * h_ref[...]
    up = jnp.where(i == 0, 0.0, gcp_ref[...] * hp_ref[...])
    w = w_ref[...]
    conv = w[0:1] * _shifted(u, up, 2) + w[1:2] * _shifted(u, up, 1) + w[2:3] * u
    o_ref[...] = (gb_ref[...] * conv).astype(o_ref.dtype)


def _sconv_prompt(proj, w, n_batch, seq, tm):
    sc = w.shape[1]
    nt = seq // tm
    cb = 3 * ATT_W // sc
    cur = lambda c: pl.BlockSpec((tm, sc), lambda b, i: (b * nt + i, cb + c))
    prev = lambda c: pl.BlockSpec((8, sc), lambda b, i: (jnp.maximum((b * nt + i) * (tm // 8) - 1, 0), cb + c))
    return pl.pallas_call(
        _sconv_prompt_kernel,
        out_shape=_sds((n_batch * seq, sc), BF16),
        grid=(n_batch, nt),
        in_specs=[cur(0), cur(1), cur(2), prev(0), prev(2), pl.BlockSpec((SC_WIDTH, sc), lambda b, i: (0, 0))],
        out_specs=pl.BlockSpec((tm, sc), lambda b, i: (b * nt + i, 0)),
        compiler_params=_cparams(("parallel", "parallel")),
        name="sconv_prompt",
    )(proj, proj, proj, proj, proj, w)


def _sconv_sample_kernel(h_ref, gb_ref, gc_ref, p0_ref, p1_ref, w_ref, o_ref, u_ref):
    u = gc_ref[...] * h_ref[...]
    w = w_ref[...]
    conv = w[0:1] * p0_ref[...] + w[1:2] * p1_ref[...] + w[2:3] * u
    o_ref[...] = (gb_ref[...] * conv).astype(o_ref.dtype)
    u_ref[...] = u


def _sconv_sample(h, gb, gc, p0, p1, w):
    nb, sc = h.shape
    return pl.pallas_call(
        _sconv_sample_kernel,
        out_shape=(_sds((nb, sc), BF16), _sds((nb, sc), F32)),
        name="sconv_sample",
    )(h, gb, gc, p0, p1, w)


def _qk_normalise(y, cb, n_qk_blocks):
    scale = jnp.where(cb < n_qk_blocks // 2, DN_DIM ** -0.5, 1.0)
    segs = []
    for s in range(y.shape[1] // DN_DIM):
        seg = y[:, s * DN_DIM:(s + 1) * DN_DIM]
        ss = jnp.sum(seg * seg, axis=-1, keepdims=True)
        segs.append(seg * (lax.rsqrt(ss + L2_EPS) * scale))
    return jnp.concatenate(segs, axis=1)


def _dconv_prompt_kernel(x_ref, xp_ref, w_ref, o_ref, *, n_qk_blocks):
    cb = pl.program_id(0)
    i = pl.program_id(2)
    x = x_ref[...]
    xp = jnp.where(i == 0, 0.0, xp_ref[...])
    w = w_ref[...]
    conv = (w[0:1] * _shifted(x, xp, 3) + w[1:2] * _shifted(x, xp, 2)
            + w[2:3] * _shifted(x, xp, 1) + w[3:4] * x)
    y = _silu(conv)

    @pl.when(cb < n_qk_blocks)
    def _():
        o_ref[...] = _qk_normalise(y, cb, n_qk_blocks)

    @pl.when(cb >= n_qk_blocks)
    def _():
        o_ref[...] = y


def _dconv_prompt(proj, w, n_batch, seq, tm, tc):
    nt = seq // tm
    ncb = DN_CONV_DIM // tc
    return pl.pallas_call(
        functools.partial(_dconv_prompt_kernel, n_qk_blocks=2 * DN_QK_W // tc),
        out_shape=_sds((n_batch * seq, DN_CONV_DIM), F32),
        grid=(ncb, n_batch, nt),
        in_specs=[pl.BlockSpec((tm, tc), lambda c, b, i: (b * nt + i, c)),
                  pl.BlockSpec((8, tc), lambda c, b, i: (jnp.maximum((b * nt + i) * (tm // 8) - 1, 0), c)),
                  pl.BlockSpec((DN_CONV_WIDTH, tc), lambda c, b, i: (0, c))],
        out_specs=pl.BlockSpec((tm, tc), lambda c, b, i: (b * nt + i, c)),
        compiler_params=_cparams(("parallel", "parallel", "parallel")),
        name="dconv_prompt",
    )(proj, proj, w)


def _dconv_sample_kernel(x_ref, p0_ref, p1_ref, p2_ref, w_ref, o_ref, *, n_qk_blocks):
    cb = pl.program_id(0)
    w = w_ref[...]
    conv = w[0:1] * p0_ref[...] + w[1:2] * p1_ref[...] + w[2:3] * p2_ref[...] + w[3:4] * x_ref[...]
    y = _silu(conv)

    @pl.when(cb < n_qk_blocks)
    def _():
        o_ref[...] = _qk_normalise(y, cb, n_qk_blocks)

    @pl.when(cb >= n_qk_blocks)
    def _():
        o_ref[...] = y


def _dconv_sample(x, p0, p1, p2, w, tc):
    nb = x.shape[0]
    blk = pl.BlockSpec((nb, tc), lambda c: (0, c))
    return pl.pallas_call(
        functools.partial(_dconv_sample_kernel, n_qk_blocks=2 * DN_QK_W // tc),
        out_shape=_sds((nb, DN_CONV_DIM), F32),
        grid=(DN_CONV_DIM // tc,),
        in_specs=[blk, blk, blk, blk, pl.BlockSpec((DN_CONV_WIDTH, tc), lambda c: (0, c))],
        out_specs=blk,
        compiler_params=_cparams(("parallel",)),
        name="dconv_sample",
    )(x, p0, p1, p2, w)


def _gated_rms(o, z, norm_w):
    return o * lax.rsqrt(jnp.mean(o * o, axis=-1, keepdims=True) + RMS_EPS) * norm_w * _silu(z)


def _delta_prompt_kernel(q_ref, k_ref, v_ref, z_ref, g_ref, gt_ref, beta_ref, nw_ref, o_ref, s_out_ref, s_ref,
                         *, n_chunks):
    c = pl.program_id(2)

    @pl.when(c == 0)
    def _():
        s_ref[...] = jnp.zeros_like(s_ref)

    cc = DN_CHUNK
    heads = range(DN_GROUP_V)
    ii = lax.broadcasted_iota(jnp.int32, (cc, cc), 0)
    jj = lax.broadcasted_iota(jnp.int32, (cc, cc), 1)
    beta_all = beta_ref[...]
    norm_w = nw_ref[...]

    lower = (ii >= jj).astype(BF16)
    gc_all = _dot_exact_lhs(lower, g_ref[...])
    kk = lax.broadcasted_iota(jnp.int32, (2 * cc, cc), 0) - lax.rem(c, 2) * cc
    tt = lax.broadcasted_iota(jnp.int32, (2 * cc, cc), 1)
    upto = jnp.logical_and(kk >= 0, kk <= tt).astype(BF16)
    gr_all = _dot_exact_rhs(gt_ref[...], upto)

    q = [q_ref[:, h * DN_DIM:(h + 1) * DN_DIM] for h in range(DN_GROUP_QK)]
    k = [k_ref[:, h * DN_DIM:(h + 1) * DN_DIM] for h in range(DN_GROUP_QK)]
    k_b = [x.astype(BF16) for x in k]
    qk = [_dot_nt(q[h].astype(BF16), k_b[h]) for h in range(DN_GROUP_QK)]

    gc = [gc_all[:, h:h + 1] for h in heads]
    beta = [beta_all[:, h:h + 1] for h in heads]
    decay = [jnp.where(ii >= jj, jnp.exp(jnp.minimum(gc[h] - gr_all[h:h + 1, :], 0.0)), 0.0) for h in heads]
    eg = [jnp.exp(gc[h]) for h in heads]
    kbeta = [k[h // 2] * beta[h] for h in heads]
    npow = [-jnp.where(ii > jj, _dot_nt(kbeta[h].astype(BF16), k_b[h // 2]) * decay[h], 0.0) for h in heads]
    y = [jnp.concatenate([v_ref[:, h * DN_DIM:(h + 1) * DN_DIM] * beta[h], kbeta[h] * eg[h]], axis=1)
         for h in heads]
    for step in range(6):
        upd = [_dot3(npow[h], y[h]) for h in heads]
        y = [y[h] + upd[h] for h in heads]
        if step < 5:
            npow = [_dot3(npow[h], npow[h]) for h in heads]
    s_old = [s_ref[h] for h in heads]
    s_b = [x.astype(BF16) for x in s_old]
    v_new = [y[h][:, :DN_DIM] - _dot(y[h][:, DN_DIM:].astype(BF16), s_b[h]) for h in heads]
    v_new_b = [x.astype(BF16) for x in v_new]
    o = [_dot((q[h // 2] * eg[h]).astype(BF16), s_b[h]) + _dot((qk[h // 2] * decay[h]).astype(BF16), v_new_b[h])
         for h in heads]
    for h in heads:
        g_last = gc[h][cc - 1:cc, :]
        k_dec = k[h // 2] * jnp.exp(g_last - gc[h])
        s_ref[h] = s_old[h] * jnp.exp(g_last) + _dot_tn(k_dec.astype(BF16), v_new_b[h])
    for h in heads:
        z = z_ref[:, h * DN_DIM:(h + 1) * DN_DIM]
        o_ref[:, h * DN_DIM:(h + 1) * DN_DIM] = _gated_rms(o[h], z, norm_w).astype(o_ref.dtype)

    @pl.when(c == n_chunks - 1)
    def _():
        s_out_ref[...] = s_ref[...]


def _delta_prompt(qkv, proj, g, g_t, beta, norm_w, n_batch, seq):
    assert seq % (2 * DN_CHUNK) == 0
    nc = seq // DN_CHUNK
    n_hg = DN_QK_HEADS // DN_GROUP_QK
    wq = DN_GROUP_QK * DN_DIM
    wv = DN_GROUP_V * DN_DIM
    row = lambda b, hg, c: b * nc + c
    return pl.pallas_call(
        functools.partial(_delta_prompt_kernel, n_chunks=nc),
        out_shape=(_sds((n_batch * seq, DN_VW), BF16), _sds((n_batch, DN_V_HEADS, DN_DIM, DN_DIM), F32)),
        grid=(n_batch, n_hg, nc),
        in_specs=[pl.BlockSpec((DN_CHUNK, wq), lambda b, hg, c: (row(b, hg, c), hg)),
                  pl.BlockSpec((DN_CHUNK, wq), lambda b, hg, c: (row(b, hg, c), DN_QK_W // wq + hg)),
                  pl.BlockSpec((DN_CHUNK, wv), lambda b, hg, c: (row(b, hg, c), 2 * DN_QK_W // wv + hg)),
                  pl.BlockSpec((DN_CHUNK, wv), lambda b, hg, c: (row(b, hg, c), DN_CONV_DIM // wv + hg)),
                  pl.BlockSpec((None, DN_CHUNK, DN_GROUP_V), lambda b, hg, c: (hg, row(b, hg, c), 0)),
                  pl.BlockSpec((None, DN_GROUP_V, 2 * DN_CHUNK), lambda b, hg, c: (hg, 0, row(b, hg, c) // 2)),
                  pl.BlockSpec((None, DN_CHUNK, DN_GROUP_V), lambda b, hg, c: (hg, row(b, hg, c), 0)),
                  pl.BlockSpec((1, DN_DIM), lambda b, hg, c: (0, 0))],
        out_specs=(pl.BlockSpec((DN_CHUNK, wv), lambda b, hg, c: (row(b, hg, c), hg)),
                   pl.BlockSpec((None, DN_GROUP_V, DN_DIM, DN_DIM), lambda b, hg, c: (b, hg, 0, 0))),
        scratch_shapes=[pltpu.VMEM((DN_GROUP_V, DN_DIM, DN_DIM), F32)],
        compiler_params=_cparams(("parallel", "parallel", "arbitrary")),
        name="delta_prompt",
    )(qkv, qkv, qkv, proj, g, g_t, beta, norm_w.reshape(1, DN_DIM))


def _delta_sample_kernel(qt_ref, kt_ref, v_ref, z_ref, g_ref, beta_ref, nw_ref, s_ref, o_ref, s_out_ref):
    norm_w = nw_ref[...]
    qt = qt_ref[...]
    kt = kt_ref[...]
    for hv in range(DN_V_HEADS):
        hq = hv // 2
        k_col = jnp.broadcast_to(kt[:, hq:hq + 1], (DN_DIM, DN_DIM))
        q_col = jnp.broadcast_to(qt[:, hq:hq + 1], (DN_DIM, DN_DIM))
        eg = jnp.exp(g_ref[hv:hv + 1, :])
        beta = beta_ref[hv:hv + 1, :]
        s = s_ref[hv]
        sk = jnp.sum(s * k_col, axis=0, keepdims=True)
        delta = beta * (v_ref[hv:hv + 1, :] - eg * sk)
        s_new = s * eg + k_col * delta
        s_out_ref[hv] = s_new
        o = jnp.sum(s_new * q_col, axis=0, keepdims=True)
        o_ref[hv:hv + 1, :] = _gated_rms(o, z_ref[hv:hv + 1, :], norm_w).astype(o_ref.dtype)


def _delta_sample(qt, kt, v, z, g_b, beta_b, norm_w, s0):
    nb = v.shape[0]
    per_b3 = lambda shape: pl.BlockSpec((None,) + shape, lambda b: (b, 0, 0))
    st = pl.BlockSpec((None, DN_V_HEADS, DN_DIM, DN_DIM), lambda b: (b, 0, 0, 0))
    return pl.pallas_call(
        _delta_sample_kernel,
        out_shape=(_sds((nb, DN_V_HEADS, DN_DIM), BF16), _sds(s0.shape, F32)),
        grid=(nb,),
        in_specs=[per_b3((DN_DIM, DN_QK_HEADS)), per_b3((DN_DIM, DN_QK_HEADS)),
                  per_b3((DN_V_HEADS, DN_DIM)), per_b3((DN_V_HEADS, DN_DIM)),
                  per_b3((DN_V_HEADS, DN_DIM)), per_b3((DN_V_HEADS, DN_DIM)),
                  pl.BlockSpec((1, DN_DIM), lambda b: (0, 0)), st],
        out_specs=(per_b3((DN_V_HEADS, DN_DIM)), st),
        compiler_params=_cparams(("parallel",)),
        name="delta_sample",
    )(qt, kt, v, z, g_b, beta_b, norm_w.reshape(1, DN_DIM), s0)


def _mixer_layer(xf, xb, w_in, sconv_w, w_out, k_buf, v_buf, sconv_prev, n_batch, seq, g1, b1):
    tp = n_batch * seq
    proj = _matmul(xb, w_in.astype(BF16), 832, 512)
    slopes = jnp.exp2(-8.0 * (jnp.arange(ATT_HEADS, dtype=F32) + 1.0) / ATT_HEADS)
    att_p = _attn_prompt(proj, slopes, n_batch, seq)
    ns = xf.shape[0] - tp
    ps = proj[tp:]
    q_s, k_s, v_s = (ps[:, i * ATT_W:(i + 1) * ATT_W] for i in range(3))
    heads = lambda a: a.reshape(ns, ATT_HEADS, HEAD_DIM)
    att_s = _attn_sample(heads(q_s), heads(k_s), heads(v_s), k_buf, v_buf, slopes)
    sc = sconv_w.shape[1]
    gc_p = _sconv_prompt(proj, sconv_w, n_batch, seq, 512)
    h_s, gb_s, gcs_s = (ps[:, 3 * ATT_W + i * sc:3 * ATT_W + (i + 1) * sc] for i in range(3))
    gc_s, u_s = _sconv_sample(h_s, gb_s, gcs_s, sconv_prev[:, 0], sconv_prev[:, 1], sconv_w)
    mixed = jnp.concatenate([jnp.concatenate([att_p, att_s.reshape(ns, ATT_W)], axis=0),
                             jnp.concatenate([gc_p, gc_s], axis=0)], axis=1)
    x1f, x1b = _matmul_residual_ln(mixed, w_out.astype(BF16), xf, g1, b1, 320, w_out.shape[0])

    keep = min(SLAB, seq)
    tail = lambda n, c0, c1: jnp.stack([proj[(b + 1) * seq - n:(b + 1) * seq, c0:c1] for b in range(n_batch)])
    k_rows_p = tail(keep, ATT_W, 2 * ATT_W).reshape(n_batch, keep, ATT_HEADS, HEAD_DIM)
    v_rows_p = tail(keep, 2 * ATT_W, 3 * ATT_W).reshape(n_batch, keep, ATT_HEADS, HEAD_DIM)
    u_tail = (tail(SC_WIDTH - 1, 3 * ATT_W + 2 * sc, 3 * ATT_W + 3 * sc)
              * tail(SC_WIDTH - 1, 3 * ATT_W, 3 * ATT_W + sc))
    sconv_new_s = jnp.stack([sconv_prev[:, 1], u_s], axis=1)
    outs_p = (k_rows_p, v_rows_p, u_tail)
    outs_s = (k_s.reshape(ns, 1, ATT_HEADS, HEAD_DIM), v_s.reshape(ns, 1, ATT_HEADS, HEAD_DIM), sconv_new_s)
    return x1f, x1b, outs_p, outs_s


def _delta_layer(xf, xb, w_in, conv_w, a_log, dt_bias, norm_w, w_out, conv_prev, s0, n_batch, seq, g1, b1):
    tp = n_batch * seq
    ns = xf.shape[0] - tp
    n_main = DN_CONV_DIM + DN_VW
    proj = _matmul(xb, w_in[:, :n_main].astype(BF16), 832, 512)
    ba = _matmul(xb, w_in[:, n_main:].astype(BF16), 832, 2 * DN_V_HEADS)
    beta = jax.nn.sigmoid(ba[:, :DN_V_HEADS])
    g = -jnp.exp(a_log.astype(F32)) * jax.nn.softplus(ba[:, DN_V_HEADS:] + dt_bias.astype(F32))

    qkv_p = _dconv_prompt(proj, conv_w, n_batch, seq, 512, 512)
    n_vg = DN_V_HEADS // DN_GROUP_V
    grp = lambda a: a[:tp].reshape(tp, n_vg, DN_GROUP_V).transpose(1, 0, 2)
    o_p, s_p = _delta_prompt(qkv_p, proj, grp(g), grp(g).transpose(0, 2, 1), grp(beta), norm_w, n_batch, seq)

    ps = proj[tp:]
    qkv_s = _dconv_sample(ps[:, :DN_CONV_DIM], conv_prev[:, 0], conv_prev[:, 1], conv_prev[:, 2], conv_w, 512)
    qt = qkv_s[:, :DN_QK_W].reshape(ns, DN_QK_HEADS, DN_DIM).transpose(0, 2, 1)
    kt = qkv_s[:, DN_QK_W:2 * DN_QK_W].reshape(ns, DN_QK_HEADS, DN_DIM).transpose(0, 2, 1)
    v_s = qkv_s[:, 2 * DN_QK_W:].reshape(ns, DN_V_HEADS, DN_DIM)
    z_s = ps[:, DN_CONV_DIM:].reshape(ns, DN_V_HEADS, DN_DIM)
    lanes = lambda a: jnp.broadcast_to(a[tp:, :, None], (ns, DN_V_HEADS, DN_DIM))
    o_s, s_s = _delta_sample(qt, kt, v_s, z_s, lanes(g), lanes(beta), norm_w, s0)

    og = jnp.concatenate([o_p, o_s.reshape(ns, DN_VW)], axis=0)
    x1f, x1b = _matmul_residual_ln(og, w_out.astype(BF16), xf, g1, b1, 320, 2048)

    dconv_new_p = jnp.stack([proj[(b + 1) * seq - (DN_CONV_WIDTH - 1):(b + 1) * seq, :DN_CONV_DIM]
                             for b in range(n_batch)])
    dconv_new_s = jnp.concatenate([conv_prev[:, 1:], ps[:, None, :DN_CONV_DIM]], axis=1)
    return x1f, x1b, (dconv_new_p, s_p), (dconv_new_s, s_s)


def _moe_block(x1f, x1b, p_b, w_router, e_bias, w_gate, w_up, w_down, ws_gate, ws_up, ws_down,
               ple_w_proj, ple_w_gate, g2, b2):
    t, d = x1f.shape
    logits = _matmul(x1b, w_router.astype(BF16), 832, N_EXPERTS)
    e_idx, wts = _route(logits, e_bias)
    row_tok, tile_expert, tile_valid, dest_of_assign = _dispatch_plan(e_idx)
    y_rows = _moe_experts(x1b[row_tok], tile_expert, tile_valid, w_gate, w_up, w_down)
    y_assign = y_rows[dest_of_assign].reshape(t, TOP_K * d)
    return _final_block(x1f, x1b, y_assign, wts, p_b, ws_gate.astype(BF16), ws_up.astype(BF16),
                        ws_down.astype(BF16), ple_w_gate.astype(BF16), ple_w_proj.astype(BF16), g2, b2, 160)


def kernel(x_prompt, x_sample, cache_attn_k, cache_attn_v, state_sconv, state_dconv, state_delta,
           p_prompt, p_sample, mix_w_in, mix_sconv_w, mix_w_out, dn_w_in, dn_conv_w, dn_a_log,
           dn_dt_bias, dn_norm_w, dn_w_out, ln1_g, ln1_b, ln2_g, ln2_b, moe_w_router, moe_e_bias,
           moe_w_gate, moe_w_up, moe_w_down, moe_ws_gate, moe_ws_up, moe_ws_down, ple_w_proj,
           ple_w_gate):
    n_batch, seq, d = x_prompt.shape
    ns = x_sample.shape[0]
    assert x_sample.shape[1] == 1
    tp = n_batch * seq
    xf = jnp.concatenate([x_prompt.reshape(tp, d), x_sample.reshape(ns, d)], axis=0)
    xb = xf.astype(BF16)
    p_all = jnp.concatenate([p_prompt.reshape(DEPTH, tp, -1), p_sample.reshape(DEPTH, ns, -1)],
                            axis=1).astype(BF16)

    mix_p, mix_s, dn_p, dn_s = [], [], [], []
    for i in range(DEPTH):
        j = i // 2
        if i % 2 == 0:
            x1f, x1b, o_p, o_s = _mixer_layer(
                xf, xb, mix_w_in[j], mix_sconv_w[j], mix_w_out[j], cache_attn_k[j], cache_attn_v[j],
                state_sconv[j], n_batch, seq, ln1_g[i], ln1_b[i])
            mix_p.append(o_p)
            mix_s.append(o_s)
        else:
            x1f, x1b, o_p, o_s = _delta_layer(
                xf, xb, dn_w_in[j], dn_conv_w[j], dn_a_log[j], dn_dt_bias[j], dn_norm_w[j], dn_w_out[j],
                state_dconv[j], state_delta[j], n_batch, seq, ln1_g[i], ln1_b[i])
            dn_p.append(o_p)
            dn_s.append(o_s)
        xf, xb = _moe_block(x1f, x1b, p_all[i], moe_w_router[i], moe_e_bias[i], moe_w_gate[i],
                            moe_w_up[i], moe_w_down[i], moe_ws_gate[i], moe_ws_up[i], moe_ws_down[i],
                            ple_w_proj[i], ple_w_gate[i], ln2_g[i], ln2_b[i])

    stack = lambda parts, idx: jnp.stack([p[idx] for p in parts])
    y_prompt = xf[:tp].reshape(n_batch, seq, d)
    y_sample = xf[tp:].reshape(ns, 1, d)
    return (y_prompt, y_sample,
            stack(mix_p, 0), stack(mix_p, 1), stack(mix_p, 2), stack(dn_p, 0), stack(dn_p, 1),
            stack(mix_s, 0), stack(mix_s, 1), stack(mix_s, 2), stack(dn_s, 0), stack(dn_s, 1))
```

```python
import functools
import math

import jax
import jax.numpy as jnp
from jax import lax
from jax.experimental import pallas as pl
from jax.experimental.pallas import tpu as pltpu

F32 = jnp.float32
BF16 = jnp.bfloat16

DEPTH = 2
HEAD_DIM = 128
ATT_HEADS = 12
ATT_W = ATT_HEADS * HEAD_DIM
SC_WIDTH = 3
DILATIONS = (1, 4, 16)
BAND = 128
SLAB = BAND * DILATIONS[-1]
ATT_SCALE = HEAD_DIM ** -0.5
DN_QK_HEADS = 16
DN_V_HEADS = 32
DN_DIM = 128
DN_QK_W = DN_QK_HEADS * DN_DIM
DN_VW = DN_V_HEADS * DN_DIM
DN_CONV_DIM = 2 * DN_QK_W + DN_VW
DN_CONV_WIDTH = 4
DN_CHUNK = 64
DN_GROUP_QK = 4
DN_GROUP_V = 2 * DN_GROUP_QK
N_EXPERTS = 64
N_GROUPS = 8
TOPK_GROUPS = 4
TOP_K = 8
ROUTED_SCALE = 2.5
ALPHA = (2 * DEPTH) ** 0.25
LN_EPS = 1e-5
RMS_EPS = 1e-6
L2_EPS = 1e-6
NEG = -1e30

MOE_TM = 256
VMEM_LIMIT = 56 * 1024 * 1024


def _sds(shape, dtype):
    return jax.ShapeDtypeStruct(shape, dtype)


def _cparams(sem):
    return pltpu.CompilerParams(dimension_semantics=sem, vmem_limit_bytes=VMEM_LIMIT)


def _silu(x):
    return x * jax.nn.sigmoid(x)


def _layer_norm_rows(v, g, b):
    mu = jnp.mean(v, axis=-1, keepdims=True)
    c = v - mu
    var = jnp.mean(c * c, axis=-1, keepdims=True)
    return c * lax.rsqrt(var + LN_EPS) * g + b


def _split3(x):
    hi = x.astype(BF16)
    r1 = x - hi.astype(F32)
    mid = r1.astype(BF16)
    lo = (r1 - mid.astype(F32)).astype(BF16)
    return hi, mid, lo


def _dot(a, b):
    return jnp.dot(a, b, preferred_element_type=F32)


def _dot_nt(a, b):
    return lax.dot_general(a, b, (((1,), (1,)), ((), ())), preferred_element_type=F32)


def _dot_tn(a, b):
    return lax.dot_general(a, b, (((0,), (0,)), ((), ())), preferred_element_type=F32)


def _dot_exact_lhs(sel_bf16, x):
    hi, mid, lo = _split3(x)
    return _dot(sel_bf16, hi) + _dot(sel_bf16, mid) + _dot(sel_bf16, lo)


def _dot_exact_rhs(x, sel_bf16):
    hi, mid, lo = _split3(x)
    return _dot(hi, sel_bf16) + _dot(mid, sel_bf16) + _dot(lo, sel_bf16)


def _dot3(a, b):
    a_hi = a.astype(BF16)
    a_lo = (a - a_hi.astype(F32)).astype(BF16)
    b_hi = b.astype(BF16)
    b_lo = (b - b_hi.astype(F32)).astype(BF16)
    return _dot(a_hi, b_hi) + _dot(a_hi, b_lo) + _dot(a_lo, b_hi)


def _mm_kernel(x_ref, w_ref, o_ref):
    o_ref[...] = _dot(x_ref[...], w_ref[...])


def _matmul(x, w, tm, tn):
    m, k = x.shape
    n = w.shape[1]
    assert m % tm == 0 and n % tn == 0
    return pl.pallas_call(
        _mm_kernel,
        out_shape=_sds((m, n), F32),
        grid=(m // tm, n // tn),
        in_specs=[pl.BlockSpec((tm, k), lambda i, j: (i, 0)),
                  pl.BlockSpec((k, tn), lambda i, j: (0, j))],
        out_specs=pl.BlockSpec((tm, tn), lambda i, j: (i, j)),
        compiler_params=_cparams(("parallel", "parallel")),
        name="matmul",
    )(x, w)


def _mm_ln_kernel(a_ref, w_ref, x_ref, g_ref, b_ref, of_ref, ob_ref, acc_ref, *, nk):
    kk = pl.program_id(1)
    part = _dot(a_ref[...], w_ref[...])

    @pl.when(kk == 0)
    def _():
        acc_ref[...] = part

    @pl.when(kk > 0)
    def _():
        acc_ref[...] += part

    @pl.when(kk == nk - 1)
    def _():
        y = _layer_norm_rows(ALPHA * x_ref[...] + acc_ref[...], g_ref[...], b_ref[...])
        of_ref[...] = y
        ob_ref[...] = y.astype(BF16)


def _matmul_residual_ln(a, w, x, g, b, tm, tk):
    m, k = a.shape
    d = w.shape[1]
    assert m % tm == 0 and k % tk == 0
    nk = k // tk
    return pl.pallas_call(
        functools.partial(_mm_ln_kernel, nk=nk),
        out_shape=(_sds((m, d), F32), _sds((m, d), BF16)),
        grid=(m // tm, nk),
        in_specs=[pl.BlockSpec((tm, tk), lambda i, kk: (i, kk)),
                  pl.BlockSpec((tk, d), lambda i, kk: (kk, 0)),
                  pl.BlockSpec((tm, d), lambda i, kk: (i, 0)),
                  pl.BlockSpec((1, d), lambda i, kk: (0, 0)),
                  pl.BlockSpec((1, d), lambda i, kk: (0, 0))],
        out_specs=(pl.BlockSpec((tm, d), lambda i, kk: (i, 0)),
                   pl.BlockSpec((tm, d), lambda i, kk: (i, 0))),
        scratch_shapes=[pltpu.VMEM((tm, d), F32)],
        compiler_params=_cparams(("parallel", "arbitrary")),
        name="matmul_residual_ln",
    )(a, w, x, g.reshape(1, d), b.reshape(1, d))


def _moe_kernel(te_ref, tv_ref, tok_ref, tok_next_ref, x_hbm, wg_ref, wu_ref, wd_ref, o_ref,
                wg_s, wu_s, wd_s, xbuf, sems):
    i = pl.program_id(0)
    n = pl.num_programs(0)
    slot = lax.rem(i, 2)
    sub = x_hbm.shape[1]
    tm = o_ref.shape[0]

    def rows_of(sl, r):
        return xbuf.at[pl.ds(pl.multiple_of((sl * tm + r) * sub, sub), sub), :]

    def start_gather(toks, sl):
        def body(r, carry):
            pltpu.make_async_copy(x_hbm.at[toks[0, r]], rows_of(sl, r), sems.at[sl]).start()
            return carry
        lax.fori_loop(0, tm, body, 0, unroll=8)

    @pl.when(jnp.logical_and(i == 0, tv_ref[0] > 0))
    def _():
        start_gather(tok_ref, 0)

    @pl.when(jnp.logical_and(i + 1 < n, tv_ref[jnp.minimum(i + 1, n - 1)] > 0))
    def _():
        start_gather(tok_next_ref, 1 - slot)

    e = te_ref[i]
    prev = te_ref[jnp.maximum(i - 1, 0)]

    @pl.when(jnp.logical_or(i == 0, e != prev))
    def _():
        wg_s[...] = wg_ref[...].astype(BF16)
        wu_s[...] = wu_ref[...].astype(BF16)
        wd_s[...] = wd_ref[...].astype(BF16)

    @pl.when(tv_ref[i] > 0)
    def _():
        whole = xbuf.at[pl.ds(pl.multiple_of(slot * tm * sub, tm * sub), tm * sub), :]
        pltpu.make_async_copy(whole, whole, sems.at[slot]).wait()
        base = slot * tm * sub
        x = jnp.concatenate([xbuf[pl.ds(base + s, tm, stride=sub), :] for s in range(sub)],
                            axis=1).astype(BF16)
        a = _dot(x, wg_s[...])
        u = _dot(x, wu_s[...])
        o_ref[...] = _dot((_silu(a) * u).astype(BF16), wd_s[...]).astype(o_ref.dtype)

    @pl.when(tv_ref[i] == 0)
    def _():
        o_ref[...] = jnp.zeros_like(o_ref)


def _moe_experts(x, row_tok, tile_expert, tile_valid, w_gate, w_up, w_down, layer):
    t, d = x.shape
    n_rows = row_tok.shape[0]
    n_tiles = n_rows // MOE_TM
    de = w_gate.shape[-1]
    sub = d // 128
    toks = row_tok.reshape(n_tiles, 1, MOE_TM)
    tok_spec = lambda off: pl.BlockSpec((None, 1, MOE_TM),
                                        lambda i, te, tv: (jnp.minimum(i + off, n_tiles - 1), 0, 0),
                                        memory_space=pltpu.SMEM)
    grid_spec = pltpu.PrefetchScalarGridSpec(
        num_scalar_prefetch=2,
        grid=(n_tiles,),
        in_specs=[tok_spec(0), tok_spec(1), pl.BlockSpec(memory_space=pl.ANY),
                  pl.BlockSpec((None, None, d, de), lambda i, te, tv: (layer, te[i], 0, 0)),
                  pl.BlockSpec((None, None, d, de), lambda i, te, tv: (layer, te[i], 0, 0)),
                  pl.BlockSpec((None, None, de, d), lambda i, te, tv: (layer, te[i], 0, 0))],
        out_specs=pl.BlockSpec((MOE_TM, d), lambda i, te, tv: (i, 0)),
        scratch_shapes=[pltpu.VMEM((d, de), BF16), pltpu.VMEM((d, de), BF16), pltpu.VMEM((de, d), BF16),
                        pltpu.VMEM((2 * MOE_TM * sub, 128), F32), pltpu.SemaphoreType.DMA((2,))],
    )
    return pl.pallas_call(
        _moe_kernel,
        out_shape=_sds((n_rows, d), BF16),
        grid_spec=grid_spec,
        compiler_params=_cparams(("arbitrary",)),
        name="moe_experts",
    )(tile_expert, tile_valid, toks, toks, x.reshape(t, sub, 128), w_gate, w_up, w_down)


def _top_k_desc(vals, k):
    n = vals.shape[-1]
    iota = lax.broadcasted_iota(jnp.int32, vals.shape, vals.ndim - 1)
    top_v, top_i = [], []
    for _ in range(k):
        m = jnp.max(vals, axis=-1, keepdims=True)
        idx = jnp.min(jnp.where(vals == m, iota, n), axis=-1, keepdims=True)
        top_v.append(m)
        top_i.append(idx)
        vals = jnp.where(iota == idx, -jnp.inf, vals)
    return jnp.concatenate(top_v, axis=-1), jnp.concatenate(top_i, axis=-1)


def _route(logits, e_bias):
    t = logits.shape[0]
    scores = jax.nn.sigmoid(logits)
    sel = scores + e_bias.astype(F32)
    per_group = sel.reshape(t, N_GROUPS, N_EXPERTS // N_GROUPS)
    group_score = _top_k_desc(per_group, 2)[0].sum(-1)
    _, g_idx = _top_k_desc(group_score, TOPK_GROUPS)
    g_mask = (g_idx[:, :, None] == jnp.arange(N_GROUPS)[None, None, :]).any(axis=1)
    e_mask = jnp.repeat(g_mask, N_EXPERTS // N_GROUPS, axis=1)
    _, e_idx = _top_k_desc(jnp.where(e_mask, sel, -jnp.inf), TOP_K)
    wts = jnp.take_along_axis(scores, e_idx, axis=1)
    wts = wts / wts.sum(-1, keepdims=True) * ROUTED_SCALE
    return e_idx, wts


def _running_count_kernel(m_ref, o_ref, carry_ref):
    i = pl.program_id(0)

    @pl.when(i == 0)
    def _():
        carry_ref[...] = jnp.zeros_like(carry_ref)

    tm = m_ref.shape[0]
    lower = (lax.broadcasted_iota(jnp.int32, (tm, tm), 0)
             >= lax.broadcasted_iota(jnp.int32, (tm, tm), 1)).astype(BF16)
    counts = _dot(lower, m_ref[...]) + carry_ref[...]
    o_ref[...] = counts
    carry_ref[...] = counts[tm - 1:tm, :]


def _running_count(member, tm):
    t, n = member.shape
    assert t % tm == 0
    return pl.pallas_call(
        _running_count_kernel,
        out_shape=_sds((t, n), F32),
        grid=(t // tm,),
        in_specs=[pl.BlockSpec((tm, n), lambda i: (i, 0))],
        out_specs=pl.BlockSpec((tm, n), lambda i: (i, 0)),
        scratch_shapes=[pltpu.VMEM((1, n), F32)],
        compiler_params=_cparams(("arbitrary",)),
        name="running_count",
    )(member)


def _dispatch_plan(e_idx):
    t = e_idx.shape[0]
    n_assign = t * TOP_K
    member = (e_idx[:, :, None] == jnp.arange(N_EXPERTS)[None, None, :]).any(axis=1)
    running = _running_count(member.astype(BF16), math.gcd(t, 832)).astype(jnp.int32)
    counts = running[-1]
    rank = jnp.take_along_axis(running, e_idx, axis=1) - 1
    padded = (counts + MOE_TM - 1) // MOE_TM * MOE_TM
    pad_end = jnp.cumsum(padded)
    pad_start = pad_end - padded
    dest_of_assign = (pad_start[e_idx] + rank).astype(jnp.int32).reshape(-1)
    n_tiles = -(-(n_assign + N_EXPERTS * (MOE_TM - 1)) // MOE_TM)
    tok_of_assign = jnp.repeat(jnp.arange(t, dtype=jnp.int32), TOP_K)
    row_tok = jnp.zeros((n_tiles * MOE_TM,), jnp.int32).at[dest_of_assign].set(tok_of_assign)
    tile_start = jnp.arange(n_tiles) * MOE_TM
    tile_expert = jnp.minimum((tile_start[:, None] >= pad_end[None, :]).sum(axis=1),
                              N_EXPERTS - 1).astype(jnp.int32)
    tile_valid = (tile_start < pad_end[-1]).astype(jnp.int32)
    last_used = tile_expert[jnp.maximum(pad_end[-1] // MOE_TM - 1, 0)]
    tile_expert = jnp.where(tile_valid > 0, tile_expert, last_used)
    return row_tok, tile_expert, tile_valid, dest_of_assign


def _final_kernel(xf_ref, xb_ref, y_ref, wt_ref, p_ref, wsg_ref, wsu_ref, wsd_ref, wpg_ref, wpp_ref,
                  g_ref, b_ref, of_ref, ob_ref):
    wt = wt_ref[...].astype(BF16).astype(F32)
    routed = wt[:, 0:1] * y_ref[0].astype(F32)
    for j in range(1, wt.shape[1]):
        routed = routed + wt[:, j:j + 1] * y_ref[j].astype(F32)
    x = xb_ref[...]
    a = _dot(x, wsg_ref[...])
    u = _dot(x, wsu_ref[...])
    shared = _dot((_silu(a) * u).astype(BF16), wsd_ref[...])
    gate = jax.nn.sigmoid(_dot(x, wpg_ref[...]))
    ple = gate * _dot(p_ref[...], wpp_ref[...])
    v = ALPHA * xf_ref[...] + (routed + shared) + ple
    y = _layer_norm_rows(v, g_ref[...], b_ref[...])
    of_ref[...] = y
    ob_ref[...] = y.astype(BF16)


def _final_block(xf, xb, y_assign, wts, p, wsg, wsu, wsd, wpg, wpp, g, b, tm):
    m, d = xf.shape
    de = wsg.shape[1]
    dp = p.shape[1]
    nk = wts.shape[1]
    assert m % tm == 0
    row = lambda i: (i, 0)
    const = lambda i: (0, 0)
    once = dict(pipeline_mode=pl.Buffered(1))
    return pl.pallas_call(
        _final_kernel,
        out_shape=(_sds((m, d), F32), _sds((m, d), BF16)),
        grid=(m // tm,),
        in_specs=[pl.BlockSpec((tm, d), row), pl.BlockSpec((tm, d), row),
                  pl.BlockSpec((nk, tm, d), lambda i: (0, i, 0)), pl.BlockSpec((tm, nk), row),
                  pl.BlockSpec((tm, dp), row),
                  pl.BlockSpec((d, de), const, **once), pl.BlockSpec((d, de), const, **once),
                  pl.BlockSpec((de, d), const, **once), pl.BlockSpec((d, d), const, **once),
                  pl.BlockSpec((dp, d), const, **once),
                  pl.BlockSpec((1, d), const), pl.BlockSpec((1, d), const)],
        out_specs=(pl.BlockSpec((tm, d), row), pl.BlockSpec((tm, d), row)),
        compiler_params=_cparams(("parallel",)),
        name="shared_ple_ln",
    )(xf, xb, y_assign, wts, p, wsg, wsu, wsd, wpg, wpp, g.reshape(1, d), b.reshape(1, d))


def _attn_prompt_kernel(slopes_ref, q_ref, k_ref, v_ref, o_ref, kk, vv, m_s, l_s, acc_s):
    h = pl.program_id(1)
    sl = pl.program_id(2)

    @pl.when(sl == 0)
    def _():
        kk[0:SLAB, :] = jnp.zeros((SLAB, HEAD_DIM), F32)
        vv[0:SLAB, :] = jnp.zeros((SLAB, HEAD_DIM), F32)

    kk[SLAB:2 * SLAB, :] = k_ref[...]
    vv[SLAB:2 * SLAB, :] = v_ref[...]

    slope = slopes_ref[h]
    ii = lax.broadcasted_iota(jnp.int32, (BAND, BAND), 0)
    jj = lax.broadcasted_iota(jnp.int32, (BAND, BAND), 1)
    dist_prev = (ii + BAND - jj).astype(F32)
    dist_cur = (ii - jj).astype(F32)

    for di, d in enumerate(DILATIONS):
        c = slope * float(d)
        bias_prev = jnp.where(jj >= ii, -c * dist_prev, NEG)
        bias_cur = jnp.where(jj <= ii, -c * dist_cur, NEG)

        def unit(u, carry, di=di, d=d, bias_prev=bias_prev, bias_cur=bias_cur):
            sp = u // d
            r = u - sp * d
            base = sp * (BAND * d) + r

            def rows(start):
                if d == 1:
                    return pl.ds(start, BAND)
                return pl.ds(start, BAND, stride=d)

            q = q_ref[rows(base), :].astype(BF16)
            kc = kk[rows(base + SLAB), :].astype(BF16)
            vc = vv[rows(base + SLAB), :].astype(BF16)
            kp = kk[rows(base + SLAB - BAND * d), :].astype(BF16)
            vp = vv[rows(base + SLAB - BAND * d), :].astype(BF16)
            pen = jnp.where(jnp.logical_and(sl == 0, sp == 0), NEG, 0.0)
            s_p = _dot_nt(q, kp) * ATT_SCALE + (bias_prev + pen)
            s_c = _dot_nt(q, kc) * ATT_SCALE + bias_cur
            m = jnp.maximum(jnp.max(s_p, axis=-1, keepdims=True), jnp.max(s_c, axis=-1, keepdims=True))
            p_p = jnp.exp(s_p - m)
            p_c = jnp.exp(s_c - m)
            l = jnp.sum(p_p, axis=-1, keepdims=True) + jnp.sum(p_c, axis=-1, keepdims=True)
            m_s[di, rows(base), :] = jnp.broadcast_to(m, (BAND, HEAD_DIM))
            l_s[di, rows(base), :] = jnp.broadcast_to(l, (BAND, HEAD_DIM))
            acc_s[di, rows(base), :] = _dot(p_p.astype(BF16), vp) + _dot(p_c.astype(BF16), vc)
            return carry

        lax.fori_loop(0, SLAB // BAND, unit, 0)

    def merge(t, carry):
        rows = pl.ds(pl.multiple_of(t * BAND, BAND), BAND)
        m_max = jnp.maximum(jnp.maximum(m_s[0, rows, :], m_s[1, rows, :]), m_s[2, rows, :])
        w0 = jnp.exp(m_s[0, rows, :] - m_max)
        num = w0 * acc_s[0, rows, :]
        den = w0 * l_s[0, rows, :]
        for di in range(1, len(DILATIONS)):
            w = jnp.exp(m_s[di, rows, :] - m_max)
            num = num + w * acc_s[di, rows, :]
            den = den + w * l_s[di, rows, :]
        o_ref[rows, :] = (num / den).astype(o_ref.dtype)
        return carry

    lax.fori_loop(0, SLAB // BAND, merge, 0)
    kk[0:SLAB, :] = kk[SLAB:2 * SLAB, :]
    vv[0:SLAB, :] = vv[SLAB:2 * SLAB, :]


def _attn_prompt(proj, slopes, n_batch, seq):
    assert seq % SLAB == 0
    ns = seq // SLAB
    grid_spec = pltpu.PrefetchScalarGridSpec(
        num_scalar_prefetch=1,
        grid=(n_batch, ATT_HEADS, ns),
        in_specs=[pl.BlockSpec((SLAB, HEAD_DIM), lambda b, h, s, sl: (b * ns + s, h)),
                  pl.BlockSpec((SLAB, HEAD_DIM), lambda b, h, s, sl: (b * ns + s, ATT_HEADS + h)),
                  pl.BlockSpec((SLAB, HEAD_DIM), lambda b, h, s, sl: (b * ns + s, 2 * ATT_HEADS + h))],
        out_specs=pl.BlockSpec((SLAB, HEAD_DIM), lambda b, h, s, sl: (b * ns + s, h)),
        scratch_shapes=[pltpu.VMEM((2 * SLAB, HEAD_DIM), F32), pltpu.VMEM((2 * SLAB, HEAD_DIM), F32),
                        pltpu.VMEM((len(DILATIONS), SLAB, HEAD_DIM), F32),
                        pltpu.VMEM((len(DILATIONS), SLAB, HEAD_DIM), F32),
                        pltpu.VMEM((len(DILATIONS), SLAB, HEAD_DIM), F32)],
    )
    return pl.pallas_call(
        _attn_prompt_kernel,
        out_shape=_sds((n_batch * seq, ATT_W), BF16),
        grid_spec=grid_spec,
        compiler_params=_cparams(("parallel", "parallel", "arbitrary")),
        name="attn_prompt",
    )(slopes, proj, proj, proj)


def _attn_sample_kernel(slopes_ref, q_ref, kn_ref, vn_ref, k_ref, v_ref, o_ref):
    h = pl.program_id(1)
    n_buf = k_ref.shape[0]
    head = pl.ds(h, 1)
    q = q_ref[head, :]
    v_self = vn_ref[head, :]
    slope = slopes_ref[h]
    s_self = jnp.sum(q * kn_ref[head, :], axis=-1, keepdims=True) * ATT_SCALE
    steps = (BAND - lax.broadcasted_iota(jnp.int32, (BAND, 1), 0)).astype(F32)
    parts = []
    for d in DILATIONS:
        start = n_buf - BAND * d
        rows = pl.ds(start, BAND) if d == 1 else pl.ds(start, BAND, stride=d)
        sc = (jnp.sum(k_ref[rows, :] * q, axis=-1, keepdims=True) * ATT_SCALE
              - slope * (steps * float(d)))
        m_d = jnp.maximum(jnp.max(sc, axis=0, keepdims=True), s_self)
        p = jnp.exp(sc - m_d)
        p_self = jnp.exp(s_self - m_d)
        l_d = jnp.sum(p, axis=0, keepdims=True) + p_self
        acc = jnp.sum(p * v_ref[rows, :], axis=0, keepdims=True) + p_self * v_self
        parts.append((m_d, l_d, acc))
    m_max = jnp.maximum(jnp.maximum(parts[0][0], parts[1][0]), parts[2][0])
    num = jnp.zeros(q.shape, F32)
    den = jnp.zeros(s_self.shape, F32)
    for m_d, l_d, acc in parts:
        w = jnp.exp(m_d - m_max)
        num = num + w * acc
        den = den + w * l_d
    o_ref[head, :] = num / den


def _attn_sample(q, k_new, v_new, k_buf, v_buf, slopes):
    nb, nh, n_buf, e = k_buf.shape
    assert n_buf >= BAND * DILATIONS[-1]
    tok = pl.BlockSpec((None, nh, e), lambda b, h, sl: (b, 0, 0))
    buf = pl.BlockSpec((None, None, n_buf, e), lambda b, h, sl: (b, h, 0, 0))
    grid_spec = pltpu.PrefetchScalarGridSpec(
        num_scalar_prefetch=1, grid=(nb, nh), in_specs=[tok, tok, tok, buf, buf], out_specs=tok)
    return pl.pallas_call(
        _attn_sample_kernel,
        out_shape=_sds((nb, nh, e), F32),
        grid_spec=grid_spec,
        compiler_params=_cparams(("parallel", "arbitrary")),
        name="attn_sample",
    )(slopes, q, k_new, v_new, k_buf, v_buf)


def _shifted(u, prev8, shift):
    row = lax.broadcasted_iota(jnp.int32, u.shape, 0)
    out = pltpu.roll(u, shift, 0)
    for s in range(shift):
        out = jnp.where(row == s, prev8[8 - shift + s:8 - shift + s + 1, :], out)
    return out


def _sconv_prompt_kernel(h_ref, gb_ref, gc_ref, hp_ref, gcp_ref, w_ref, o_ref):
    i = pl.program_id(1)
    u = gc_ref[...] * h_ref[...]
    up = jnp.where(i == 0, 0.0, gcp_ref[...] * hp_ref[...])
    w = w_ref[...]
    conv = w[0:1] * _shifted(u, up, 2) + w[1:2] * _shifted(u, up, 1) + w[2:3] * u
    o_ref[...] = (gb_ref[...] * conv).astype(o_ref.dtype)


def _sconv_prompt(proj, w, n_batch, seq, tm):
    sc = w.shape[1]
    nt = seq // tm
    cb = 3 * ATT_W // sc
    cur = lambda c: pl.BlockSpec((tm, sc), lambda b, i: (b * nt + i, cb + c))
    prev = lambda c: pl.BlockSpec((8, sc), lambda b, i: (jnp.maximum((b * nt + i) * (tm // 8) - 1, 0), cb + c))
    return pl.pallas_call(
        _sconv_prompt_kernel,
        out_shape=_sds((n_batch * seq, sc), BF16),
        grid=(n_batch, nt),
        in_specs=[cur(0), cur(1), cur(2), prev(0), prev(2), pl.BlockSpec((SC_WIDTH, sc), lambda b, i: (0, 0))],
        out_specs=pl.BlockSpec((tm, sc), lambda b, i: (b * nt + i, 0)),
        compiler_params=_cparams(("parallel", "parallel")),
        name="sconv_prompt",
    )(proj, proj, proj, proj, proj, w)


def _sconv_sample_kernel(h_ref, gb_ref, gc_ref, p0_ref, p1_ref, w_ref, o_ref, u_ref):
    u = gc_ref[...] * h_ref[...]
    w = w_ref[...]
    conv = w[0:1] * p0_ref[...] + w[1:2] * p1_ref[...] + w[2:3] * u
    o_ref[...] = (gb_ref[...] * conv).astype(o_ref.dtype)
    u_ref[...] = u


def _sconv_sample(h, gb, gc, p0, p1, w):
    nb, sc = h.shape
    return pl.pallas_call(
        _sconv_sample_kernel,
        out_shape=(_sds((nb, sc), BF16), _sds((nb, sc), F32)),
        name="sconv_sample",
    )(h, gb, gc, p0, p1, w)


def _qk_normalise(y, cb, n_qk_blocks):
    scale = jnp.where(cb < n_qk_blocks // 2, DN_DIM ** -0.5, 1.0)
    segs = []
    for s in range(y.shape[1] // DN_DIM):
        seg = y[:, s * DN_DIM:(s + 1) * DN_DIM]
        ss = jnp.sum(seg * seg, axis=-1, keepdims=True)
        segs.append(seg * (lax.rsqrt(ss + L2_EPS) * scale))
    return jnp.concatenate(segs, axis=1)


def _dconv_prompt_kernel(x_ref, xp_ref, w_ref, o_ref, *, n_qk_blocks):
    cb = pl.program_id(0)
    i = pl.program_id(2)
    x = x_ref[...]
    xp = jnp.where(i == 0, 0.0, xp_ref[...])
    w = w_ref[...]
    conv = (w[0:1] * _shifted(x, xp, 3) + w[1:2] * _shifted(x, xp, 2)
            + w[2:3] * _shifted(x, xp, 1) + w[3:4] * x)
    y = _silu(conv)

    @pl.when(cb < n_qk_blocks)
    def _():
        o_ref[...] = _qk_normalise(y, cb, n_qk_blocks)

    @pl.when(cb >= n_qk_blocks)
    def _():
        o_ref[...] = y


def _dconv_prompt(proj, w, n_batch, seq, tm, tc):
    nt = seq // tm
    ncb = DN_CONV_DIM // tc
    return pl.pallas_call(
        functools.partial(_dconv_prompt_kernel, n_qk_blocks=2 * DN_QK_W // tc),
        out_shape=_sds((n_batch * seq, DN_CONV_DIM), F32),
        grid=(ncb, n_batch, nt),
        in_specs=[pl.BlockSpec((tm, tc), lambda c, b, i: (b * nt + i, c)),
                  pl.BlockSpec((8, tc), lambda c, b, i: (jnp.maximum((b * nt + i) * (tm // 8) - 1, 0), c)),
                  pl.BlockSpec((DN_CONV_WIDTH, tc), lambda c, b, i: (0, c))],
        out_specs=pl.BlockSpec((tm, tc), lambda c, b, i: (b * nt + i, c)),
        compiler_params=_cparams(("parallel", "parallel", "parallel")),
        name="dconv_prompt",
    )(proj, proj, w)


def _dconv_sample_kernel(x_ref, p0_ref, p1_ref, p2_ref, w_ref, o_ref, *, n_qk_blocks):
    cb = pl.program_id(0)
    w = w_ref[...]
    conv = w[0:1] * p0_ref[...] + w[1:2] * p1_ref[...] + w[2:3] * p2_ref[...] + w[3:4] * x_ref[...]
    y = _silu(conv)

    @pl.when(cb < n_qk_blocks)
    def _():
        o_ref[...] = _qk_normalise(y, cb, n_qk_blocks)

    @pl.when(cb >= n_qk_blocks)
    def _():
        o_ref[...] = y


def _dconv_sample(x, p0, p1, p2, w, tc):
    nb = x.shape[0]
    blk = pl.BlockSpec((nb, tc), lambda c: (0, c))
    return pl.pallas_call(
        functools.partial(_dconv_sample_kernel, n_qk_blocks=2 * DN_QK_W // tc),
        out_shape=_sds((nb, DN_CONV_DIM), F32),
        grid=(DN_CONV_DIM // tc,),
        in_specs=[blk, blk, blk, blk, pl.BlockSpec((DN_CONV_WIDTH, tc), lambda c: (0, c))],
        out_specs=blk,
        compiler_params=_cparams(("parallel",)),
        name="dconv_sample",
    )(x, p0, p1, p2, w)


def _gated_rms(o, z, norm_w):
    return o * lax.rsqrt(jnp.mean(o * o, axis=-1, keepdims=True) + RMS_EPS) * norm_w * _silu(z)


def _delta_prompt_kernel(q_ref, k_ref, v_ref, z_ref, g_ref, gt_ref, beta_ref, nw_ref, o_ref, s_out_ref, s_ref,
                         *, n_chunks):
    c = pl.program_id(2)

    @pl.when(c == 0)
    def _():
        s_ref[...] = jnp.zeros_like(s_ref)

    cc = DN_CHUNK
    heads = range(DN_GROUP_V)
    ii = lax.broadcasted_iota(jnp.int32, (cc, cc), 0)
    jj = lax.broadcasted_iota(jnp.int32, (cc, cc), 1)
    beta_all = beta_ref[...]
    norm_w = nw_ref[...]

    lower = (ii >= jj).astype(BF16)
    gc_all = _dot_exact_lhs(lower, g_ref[...])
    kk = lax.broadcasted_iota(jnp.int32, (2 * cc, cc), 0) - lax.rem(c, 2) * cc
    tt = lax.broadcasted_iota(jnp.int32, (2 * cc, cc), 1)
    upto = jnp.logical_and(kk >= 0, kk <= tt).astype(BF16)
    gr_all = _dot_exact_rhs(gt_ref[...], upto)

    q = [q_ref[:, h * DN_DIM:(h + 1) * DN_DIM] for h in range(DN_GROUP_QK)]
    k = [k_ref[:, h * DN_DIM:(h + 1) * DN_DIM] for h in range(DN_GROUP_QK)]
    k_b = [x.astype(BF16) for x in k]
    qk = [_dot_nt(q[h].astype(BF16), k_b[h]) for h in range(DN_GROUP_QK)]

    gc = [gc_all[:, h:h + 1] for h in heads]
    beta = [beta_all[:, h:h + 1] for h in heads]
    decay = [jnp.where(ii >= jj, jnp.exp(jnp.minimum(gc[h] - gr_all[h:h + 1, :], 0.0)), 0.0) for h in heads]
    eg = [jnp.exp(gc[h]) for h in heads]
    kbeta = [k[h // 2] * beta[h] for h in heads]
    npow = [-jnp.where(ii > jj, _dot_nt(kbeta[h].astype(BF16), k_b[h // 2]) * decay[h], 0.0) for h in heads]
    y = [jnp.concatenate([v_ref[:, h * DN_DIM:(h + 1) * DN_DIM] * beta[h], kbeta[h] * eg[h]], axis=1)
         for h in heads]
    for step in range(6):
        upd = [_dot3(npow[h], y[h]) for h in heads]
        y = [y[h] + upd[h] for h in heads]
        if step < 5:
            npow = [_dot3(npow[h], npow[h]) for h in heads]
    s_old = [s_ref[h] for h in heads]
    s_b = [x.astype(BF16) for x in s_old]
    v_new = [y[h][:, :DN_DIM] - _dot(y[h][:, DN_DIM:].astype(BF16), s_b[h]) for h in heads]
    v_new_b = [x.astype(BF16) for x in v_new]
    o = [_dot((q[h // 2] * eg[h]).astype(BF16), s_b[h]) + _dot((qk[h // 2] * decay[h]).astype(BF16), v_new_b[h])
         for h in heads]
    for h in heads:
        g_last = gc[h][cc - 1:cc, :]
        k_dec = k[h // 2] * jnp.exp(g_last - gc[h])
        s_ref[h] = s_old[h] * jnp.exp(g_last) + _dot_tn(k_dec.astype(BF16), v_new_b[h])
    for h in heads:
        z = z_ref[:, h * DN_DIM:(h + 1) * DN_DIM]
        o_ref[:, h * DN_DIM:(h + 1) * DN_DIM] = _gated_rms(o[h], z, norm_w).astype(o_ref.dtype)

    @pl.when(c == n_chunks - 1)
    def _():
        s_out_ref[...] = s_ref[...]


def _delta_prompt(qkv, proj, g, g_t, beta, norm_w, n_batch, seq):
    assert seq % (2 * DN_CHUNK) == 0
    nc = seq // DN_CHUNK
    n_hg = DN_QK_HEADS // DN_GROUP_QK
    wq = DN_GROUP_QK * DN_DIM
    wv = DN_GROUP_V * DN_DIM
    row = lambda b, hg, c: b * nc + c
    return pl.pallas_call(
        functools.partial(_delta_prompt_kernel, n_chunks=nc),
        out_shape=(_sds((n_batch * seq, DN_VW), BF16), _sds((n_batch, DN_V_HEADS, DN_DIM, DN_DIM), F32)),
        grid=(n_batch, n_hg, nc),
        in_specs=[pl.BlockSpec((DN_CHUNK, wq), lambda b, hg, c: (row(b, hg, c), hg)),
                  pl.BlockSpec((DN_CHUNK, wq), lambda b, hg, c: (row(b, hg, c), DN_QK_W // wq + hg)),
                  pl.BlockSpec((DN_CHUNK, wv), lambda b, hg, c: (row(b, hg, c), 2 * DN_QK_W // wv + hg)),
                  pl.BlockSpec((DN_CHUNK, wv), lambda b, hg, c: (row(b, hg, c), DN_CONV_DIM // wv + hg)),
                  pl.BlockSpec((None, DN_CHUNK, DN_GROUP_V), lambda b, hg, c: (hg, row(b, hg, c), 0)),
                  pl.BlockSpec((None, DN_GROUP_V, 2 * DN_CHUNK), lambda b, hg, c: (hg, 0, row(b, hg, c) // 2)),
                  pl.BlockSpec((None, DN_CHUNK, DN_GROUP_V), lambda b, hg, c: (hg, row(b, hg, c), 0)),
                  pl.BlockSpec((1, DN_DIM), lambda b, hg, c: (0, 0))],
        out_specs=(pl.BlockSpec((DN_CHUNK, wv), lambda b, hg, c: (row(b, hg, c), hg)),
                   pl.BlockSpec((None, DN_GROUP_V, DN_DIM, DN_DIM), lambda b, hg, c: (b, hg, 0, 0))),
        scratch_shapes=[pltpu.VMEM((DN_GROUP_V, DN_DIM, DN_DIM), F32)],
        compiler_params=_cparams(("parallel", "parallel", "arbitrary")),
        name="delta_prompt",
    )(qkv, qkv, qkv, proj, g, g_t, beta, norm_w.reshape(1, DN_DIM))


def _delta_sample_kernel(qt_ref, kt_ref, v_ref, z_ref, g_ref, beta_ref, nw_ref, s_ref, o_ref, s_out_ref):
    norm_w = nw_ref[...]
    qt = qt_ref[...]
    kt = kt_ref[...]
    for hv in range(DN_V_HEADS):
        hq = hv // 2
        k_col = jnp.broadcast_to(kt[:, hq:hq + 1], (DN_DIM, DN_DIM))
        q_col = jnp.broadcast_to(qt[:, hq:hq + 1], (DN_DIM, DN_DIM))
        eg = jnp.exp(g_ref[hv:hv + 1, :])
        beta = beta_ref[hv:hv + 1, :]
        s = s_ref[hv]
        sk = jnp.sum(s * k_col, axis=0, keepdims=True)
        delta = beta * (v_ref[hv:hv + 1, :] - eg * sk)
        s_new = s * eg + k_col * delta
        s_out_ref[hv] = s_new
        o = jnp.sum(s_new * q_col, axis=0, keepdims=True)
        o_ref[hv:hv + 1, :] = _gated_rms(o, z_ref[hv:hv + 1, :], norm_w).astype(o_ref.dtype)


def _delta_sample(qt, kt, v, z, g_b, beta_b, norm_w, s0):
    nb = v.shape[0]
    per_b3 = lambda shape: pl.BlockSpec((None,) + shape, lambda b: (b, 0, 0))
    st = pl.BlockSpec((None, DN_V_HEADS, DN_DIM, DN_DIM), lambda b: (b, 0, 0, 0))
    return pl.pallas_call(
        _delta_sample_kernel,
        out_shape=(_sds((nb, DN_V_HEADS, DN_DIM), BF16), _sds(s0.shape, F32)),
        grid=(nb,),
        in_specs=[per_b3((DN_DIM, DN_QK_HEADS)), per_b3((DN_DIM, DN_QK_HEADS)),
                  per_b3((DN_V_HEADS, DN_DIM)), per_b3((DN_V_HEADS, DN_DIM)),
                  per_b3((DN_V_HEADS, DN_DIM)), per_b3((DN_V_HEADS, DN_DIM)),
                  pl.BlockSpec((1, DN_DIM), lambda b: (0, 0)), st],
        out_specs=(per_b3((DN_V_HEADS, DN_DIM)), st),
        compiler_params=_cparams(("parallel",)),
        name="delta_sample",
    )(qt, kt, v, z, g_b, beta_b, norm_w.reshape(1, DN_DIM), s0)


def _mixer_layer(xf, xb, w_in, sconv_w, w_out, k_buf, v_buf, sconv_prev, n_batch, seq, g1, b1):
    tp = n_batch * seq
    proj = _matmul(xb, w_in.astype(BF16), 832, 512)
    slopes = jnp.exp2(-8.0 * (jnp.arange(ATT_HEADS, dtype=F32) + 1.0) / ATT_HEADS)
    att_p = _attn_prompt(proj, slopes, n_batch, seq)
    ns = xf.shape[0] - tp
    ps = proj[tp:]
    q_s, k_s, v_s = (ps[:, i * ATT_W:(i + 1) * ATT_W] for i in range(3))
    heads = lambda a: a.reshape(ns, ATT_HEADS, HEAD_DIM)
    att_s = _attn_sample(heads(q_s), heads(k_s), heads(v_s), k_buf.transpose(0, 2, 1, 3),
                         v_buf.transpose(0, 2, 1, 3), slopes).astype(BF16)
    sc = sconv_w.shape[1]
    gc_p = _sconv_prompt(proj, sconv_w, n_batch, seq, 512)
    h_s, gb_s, gcs_s = (ps[:, 3 * ATT_W + i * sc:3 * ATT_W + (i + 1) * sc] for i in range(3))
    gc_s, u_s = _sconv_sample(h_s, gb_s, gcs_s, sconv_prev[:, 0], sconv_prev[:, 1], sconv_w)
    mixed = jnp.concatenate([jnp.concatenate([att_p, att_s.reshape(ns, ATT_W)], axis=0),
                             jnp.concatenate([gc_p, gc_s], axis=0)], axis=1)
    x1f, x1b = _matmul_residual_ln(mixed, w_out.astype(BF16), xf, g1, b1, 320, w_out.shape[0])

    keep = min(SLAB, seq)
    tail = lambda n, c0, c1: jnp.stack([proj[(b + 1) * seq - n:(b + 1) * seq, c0:c1] for b in range(n_batch)])
    k_rows_p = tail(keep, ATT_W, 2 * ATT_W).reshape(n_batch, keep, ATT_HEADS, HEAD_DIM)
    v_rows_p = tail(keep, 2 * ATT_W, 3 * ATT_W).reshape(n_batch, keep, ATT_HEADS, HEAD_DIM)
    u_tail = (tail(SC_WIDTH - 1, 3 * ATT_W + 2 * sc, 3 * ATT_W + 3 * sc)
              * tail(SC_WIDTH - 1, 3 * ATT_W, 3 * ATT_W + sc))
    sconv_new_s = jnp.stack([sconv_prev[:, 1], u_s], axis=1)
    outs_p = (k_rows_p, v_rows_p, u_tail)
    outs_s = (k_s.reshape(ns, 1, ATT_HEADS, HEAD_DIM), v_s.reshape(ns, 1, ATT_HEADS, HEAD_DIM), sconv_new_s)
    return x1f, x1b, outs_p, outs_s


def _delta_layer(xf, xb, w_in, conv_w, a_log, dt_bias, norm_w, w_out, conv_prev, s0, n_batch, seq, g1, b1):
    tp = n_batch * seq
    ns = xf.shape[0] - tp
    n_main = DN_CONV_DIM + DN_VW
    proj = _matmul(xb, w_in[:, :n_main].astype(BF16), 832, 512)
    ba = _matmul(xb, w_in[:, n_main:].astype(BF16), 832, 2 * DN_V_HEADS)
    beta = jax.nn.sigmoid(ba[:, :DN_V_HEADS])
    g = -jnp.exp(a_log.astype(F32)) * jax.nn.softplus(ba[:, DN_V_HEADS:] + dt_bias.astype(F32))

    qkv_p = _dconv_prompt(proj, conv_w, n_batch, seq, 512, 512)
    n_vg = DN_V_HEADS // DN_GROUP_V
    grp = lambda a: a[:tp].reshape(tp, n_vg, DN_GROUP_V).transpose(1, 0, 2)
    o_p, s_p = _delta_prompt(qkv_p, proj, grp(g), grp(g).transpose(0, 2, 1), grp(beta), norm_w, n_batch, seq)

    ps = proj[tp:]
    qkv_s = _dconv_sample(ps[:, :DN_CONV_DIM], conv_prev[:, 0], conv_prev[:, 1], conv_prev[:, 2], conv_w, 512)
    qt = qkv_s[:, :DN_QK_W].reshape(ns, DN_QK_HEADS, DN_DIM).transpose(0, 2, 1)
    kt = qkv_s[:, DN_QK_W:2 * DN_QK_W].reshape(ns, DN_QK_HEADS, DN_DIM).transpose(0, 2, 1)
    v_s = qkv_s[:, 2 * DN_QK_W:].reshape(ns, DN_V_HEADS, DN_DIM)
    z_s = ps[:, DN_CONV_DIM:].reshape(ns, DN_V_HEADS, DN_DIM)
    lanes = lambda a: jnp.broadcast_to(a[tp:, :, None], (ns, DN_V_HEADS, DN_DIM))
    o_s, s_s = _delta_sample(qt, kt, v_s, z_s, lanes(g), lanes(beta), norm_w, s0)

    og = jnp.concatenate([o_p, o_s.reshape(ns, DN_VW)], axis=0)
    x1f, x1b = _matmul_residual_ln(og, w_out.astype(BF16), xf, g1, b1, 320, 2048)

    dconv_new_p = jnp.stack([proj[(b + 1) * seq - (DN_CONV_WIDTH - 1):(b + 1) * seq, :DN_CONV_DIM]
                             for b in range(n_batch)])
    dconv_new_s = jnp.concatenate([conv_prev[:, 1:], ps[:, None, :DN_CONV_DIM]], axis=1)
    return x1f, x1b, (dconv_new_p, s_p), (dconv_new_s, s_s)


def _moe_block(x1f, x1b, p_b, w_router, e_bias, w_gate, w_up, w_down, layer, ws_gate, ws_up, ws_down,
               ple_w_proj, ple_w_gate, g2, b2):
    t, d = x1f.shape
    logits = _matmul(x1b, w_router.astype(BF16), 832, N_EXPERTS)
    e_idx, wts = _route(logits, e_bias)
    row_tok, tile_expert, tile_valid, dest_of_assign = _dispatch_plan(e_idx)
    y_rows = _moe_experts(x1f, row_tok, tile_expert, tile_valid, w_gate, w_up, w_down, layer)
    y_assign = y_rows[dest_of_assign.reshape(t, TOP_K).T.reshape(-1)].reshape(TOP_K, t, d)
    return _final_block(x1f, x1b, y_assign, wts, p_b, ws_gate.astype(BF16), ws_up.astype(BF16),
                        ws_down.astype(BF16), ple_w_gate.astype(BF16), ple_w_proj.astype(BF16), g2, b2, 160)


def kernel(x_prompt, x_sample, cache_attn_k, cache_attn_v, state_sconv, state_dconv, state_delta,
           p_prompt, p_sample, mix_w_in, mix_sconv_w, mix_w_out, dn_w_in, dn_conv_w, dn_a_log,
           dn_dt_bias, dn_norm_w, dn_w_out, ln1_g, ln1_b, ln2_g, ln2_b, moe_w_router, moe_e_bias,
           moe_w_gate, moe_w_up, moe_w_down, moe_ws_gate, moe_ws_up, moe_ws_down, ple_w_proj,
           ple_w_gate):
    n_batch, seq, d = x_prompt.shape
    ns = x_sample.shape[0]
    assert x_sample.shape[1] == 1
    tp = n_batch * seq
    xf = jnp.concatenate([x_prompt.reshape(tp, d), x_sample.reshape(ns, d)], axis=0)
    xb = xf.astype(BF16)
    p_all = jnp.concatenate([p_prompt.reshape(DEPTH, tp, -1), p_sample.reshape(DEPTH, ns, -1)],
                            axis=1).astype(BF16)

    mix_p, mix_s, dn_p, dn_s = [], [], [], []
    for i in range(DEPTH):
        j = i // 2
        if i % 2 == 0:
            x1f, x1b, o_p, o_s = _mixer_layer(
                xf, xb, mix_w_in[j], mix_sconv_w[j], mix_w_out[j], cache_attn_k[j], cache_attn_v[j],
                state_sconv[j], n_batch, seq, ln1_g[i], ln1_b[i])
            mix_p.append(o_p)
            mix_s.append(o_s)
        else:
            x1f, x1b, o_p, o_s = _delta_layer(
                xf, xb, dn_w_in[j], dn_conv_w[j], dn_a_log[j], dn_dt_bias[j], dn_norm_w[j], dn_w_out[j],
                state_dconv[j], state_delta[j], n_batch, seq, ln1_g[i], ln1_b[i])
            dn_p.append(o_p)
            dn_s.append(o_s)
        xf, xb = _moe_block(x1f, x1b, p_all[i], moe_w_router[i], moe_e_bias[i], moe_w_gate,
                            moe_w_up, moe_w_down, i, moe_ws_gate[i], moe_ws_up[i], moe_ws_down[i],
                            ple_w_proj[i], ple_w_gate[i], ln2_g[i], ln2_b[i])

    stack = lambda parts, idx: jnp.stack([p[idx] for p in parts])
    y_prompt = xf[:tp].reshape(n_batch, seq, d)
    y_sample = xf[tp:].reshape(ns, 1, d)
    return (y_prompt, y_sample,
            stack(mix_p, 0), stack(mix_p, 1), stack(mix_p, 2), stack(dn_p, 0), stack(dn_p, 1),
            stack(mix_s, 0), stack(mix_s, 1), stack(mix_s, 2), stack(dn_s, 0), stack(dn_s, 1))
```

```python
import functools
import math

import jax
import jax.numpy as jnp
from jax import lax
from jax.experimental import pallas as pl
from jax.experimental.pallas import tpu as pltpu

F32 = jnp.float32
BF16 = jnp.bfloat16

DEPTH = 2
HEAD_DIM = 128
ATT_HEADS = 12
ATT_W = ATT_HEADS * HEAD_DIM
SC_WIDTH = 3
DILATIONS = (1, 4, 16)
BAND = 128
SLAB = BAND * DILATIONS[-1]
ATT_SCALE = HEAD_DIM ** -0.5
ATT_UNROLL = 4
ATT_SAMPLE_HEADS = 4
DN_QK_HEADS = 16
DN_V_HEADS = 32
DN_DIM = 128
DN_QK_W = DN_QK_HEADS * DN_DIM
DN_VW = DN_V_HEADS * DN_DIM
DN_CONV_DIM = 2 * DN_QK_W + DN_VW
DN_CONV_WIDTH = 4
DN_CHUNK = 64
DN_GROUP_QK = 4
DN_GROUP_V = 2 * DN_GROUP_QK
N_EXPERTS = 64
N_GROUPS = 8
TOPK_GROUPS = 4
TOP_K = 8
ROUTED_SCALE = 2.5
ALPHA = (2 * DEPTH) ** 0.25
LN_EPS = 1e-5
RMS_EPS = 1e-6
L2_EPS = 1e-6
NEG = -1e30

MOE_TM = 256
VMEM_LIMIT = 56 * 1024 * 1024


def _sds(shape, dtype):
    return jax.ShapeDtypeStruct(shape, dtype)


def _cparams(sem):
    return pltpu.CompilerParams(dimension_semantics=sem, vmem_limit_bytes=VMEM_LIMIT)


def _silu(x):
    return x * jax.nn.sigmoid(x)


def _layer_norm_rows(v, g, b):
    mu = jnp.mean(v, axis=-1, keepdims=True)
    c = v - mu
    var = jnp.mean(c * c, axis=-1, keepdims=True)
    return c * lax.rsqrt(var + LN_EPS) * g + b


def _split3(x):
    hi = x.astype(BF16)
    r1 = x - hi.astype(F32)
    mid = r1.astype(BF16)
    lo = (r1 - mid.astype(F32)).astype(BF16)
    return hi, mid, lo


def _dot(a, b):
    return jnp.dot(a, b, preferred_element_type=F32)


def _dot_nt(a, b):
    return lax.dot_general(a, b, (((1,), (1,)), ((), ())), preferred_element_type=F32)


def _dot_tn(a, b):
    return lax.dot_general(a, b, (((0,), (0,)), ((), ())), preferred_element_type=F32)


def _dot_exact_lhs(sel_bf16, x):
    hi, mid, lo = _split3(x)
    return _dot(sel_bf16, hi) + _dot(sel_bf16, mid) + _dot(sel_bf16, lo)


def _dot_exact_rhs(x, sel_bf16):
    hi, mid, lo = _split3(x)
    return _dot(hi, sel_bf16) + _dot(mid, sel_bf16) + _dot(lo, sel_bf16)


def _dot3(a, b):
    a_hi = a.astype(BF16)
    a_lo = (a - a_hi.astype(F32)).astype(BF16)
    b_hi = b.astype(BF16)
    b_lo = (b - b_hi.astype(F32)).astype(BF16)
    return _dot(a_hi, b_hi) + _dot(a_hi, b_lo) + _dot(a_lo, b_hi)


def _mm_kernel(x_ref, w_ref, o_ref):
    o_ref[...] = _dot(x_ref[...], w_ref[...])


def _matmul(x, w, tm, tn):
    m, k = x.shape
    n = w.shape[1]
    assert m % tm == 0 and n % tn == 0
    return pl.pallas_call(
        _mm_kernel,
        out_shape=_sds((m, n), F32),
        grid=(m // tm, n // tn),
        in_specs=[pl.BlockSpec((tm, k), lambda i, j: (i, 0)),
                  pl.BlockSpec((k, tn), lambda i, j: (0, j))],
        out_specs=pl.BlockSpec((tm, tn), lambda i, j: (i, j)),
        compiler_params=_cparams(("parallel", "parallel")),
        name="matmul",
    )(x, w)


def _mm_ln_kernel(a_ref, w_ref, x_ref, g_ref, b_ref, of_ref, ob_ref, acc_ref, *, nk):
    kk = pl.program_id(1)
    part = _dot(a_ref[...], w_ref[...])

    @pl.when(kk == 0)
    def _():
        acc_ref[...] = part

    @pl.when(kk > 0)
    def _():
        acc_ref[...] += part

    @pl.when(kk == nk - 1)
    def _():
        y = _layer_norm_rows(ALPHA * x_ref[...] + acc_ref[...], g_ref[...], b_ref[...])
        of_ref[...] = y
        ob_ref[...] = y.astype(BF16)


def _matmul_residual_ln(a, w, x, g, b, tm, tk):
    m, k = a.shape
    d = w.shape[1]
    assert m % tm == 0 and k % tk == 0
    nk = k // tk
    return pl.pallas_call(
        functools.partial(_mm_ln_kernel, nk=nk),
        out_shape=(_sds((m, d), F32), _sds((m, d), BF16)),
        grid=(m // tm, nk),
        in_specs=[pl.BlockSpec((tm, tk), lambda i, kk: (i, kk)),
                  pl.BlockSpec((tk, d), lambda i, kk: (kk, 0)),
                  pl.BlockSpec((tm, d), lambda i, kk: (i, 0)),
                  pl.BlockSpec((1, d), lambda i, kk: (0, 0)),
                  pl.BlockSpec((1, d), lambda i, kk: (0, 0))],
        out_specs=(pl.BlockSpec((tm, d), lambda i, kk: (i, 0)),
                   pl.BlockSpec((tm, d), lambda i, kk: (i, 0))),
        scratch_shapes=[pltpu.VMEM((tm, d), F32)],
        compiler_params=_cparams(("parallel", "arbitrary")),
        name="matmul_residual_ln",
    )(a, w, x, g.reshape(1, d), b.reshape(1, d))


def _moe_kernel(te_ref, tv_ref, tok_ref, tok_next_ref, x_hbm, wg_ref, wu_ref, wd_ref, o_ref,
                wg_s, wu_s, wd_s, xbuf, sems):
    i = pl.program_id(0)
    n = pl.num_programs(0)
    slot = lax.rem(i, 2)
    sub = x_hbm.shape[1]
    tm = o_ref.shape[0]

    def rows_of(sl, r):
        return xbuf.at[pl.ds(pl.multiple_of((sl * tm + r) * sub, sub), sub), :]

    def wait_gather(sl):
        whole = xbuf.at[pl.ds(pl.multiple_of(sl * tm * sub, tm * sub), tm * sub), :]
        pltpu.make_async_copy(whole, whole, sems.at[sl]).wait()

    @pl.when(jnp.logical_and(i == 0, tv_ref[0] > 0))
    def _():
        def body(r, carry):
            pltpu.make_async_copy(x_hbm.at[tok_ref[0, r]], rows_of(0, r), sems.at[0]).start()
            return carry
        lax.fori_loop(0, tm, body, 0, unroll=8)

    e = te_ref[i]
    prev = te_ref[jnp.maximum(i - 1, 0)]

    @pl.when(jnp.logical_or(i == 0, e != prev))
    def _():
        wg_s[...] = wg_ref[...].astype(BF16)
        wu_s[...] = wu_ref[...].astype(BF16)
        wd_s[...] = wd_ref[...].astype(BF16)

    @pl.when(tv_ref[i] > 0)
    def _():
        wait_gather(slot)
        for r in range(tm):
            pltpu.make_async_copy(x_hbm.at[tok_next_ref[0, r]], rows_of(1 - slot, r), sems.at[1 - slot]).start()
        base = slot * tm * sub
        x = jnp.concatenate([xbuf[pl.ds(base + s, tm, stride=sub), :] for s in range(sub)],
                            axis=1).astype(BF16)
        a = _dot(x, wg_s[...])
        u = _dot(x, wu_s[...])
        o_ref[...] = _dot((_silu(a) * u).astype(BF16), wd_s[...]).astype(o_ref.dtype)

    @pl.when(tv_ref[i] == 0)
    def _():
        o_ref[...] = jnp.zeros_like(o_ref)

    @pl.when(jnp.logical_and(tv_ref[i] == 0, tv_ref[jnp.maximum(i - 1, 0)] > 0))
    def _():
        wait_gather(slot)


def _moe_experts(x, row_tok, tile_expert, tile_valid, w_gate, w_up, w_down, layer):
    t, d = x.shape
    n_rows = row_tok.shape[0]
    n_tiles = n_rows // MOE_TM
    de = w_gate.shape[-1]
    sub = d // 128
    toks = row_tok.reshape(n_tiles, 1, MOE_TM)
    tok_spec = lambda off: pl.BlockSpec((None, 1, MOE_TM),
                                        lambda i, te, tv: (jnp.minimum(i + off, n_tiles - 1), 0, 0),
                                        memory_space=pltpu.SMEM)
    grid_spec = pltpu.PrefetchScalarGridSpec(
        num_scalar_prefetch=2,
        grid=(n_tiles,),
        in_specs=[tok_spec(0), tok_spec(1), pl.BlockSpec(memory_space=pl.ANY),
                  pl.BlockSpec((None, None, d, de), lambda i, te, tv: (layer, te[i], 0, 0)),
                  pl.BlockSpec((None, None, d, de), lambda i, te, tv: (layer, te[i], 0, 0)),
                  pl.BlockSpec((None, None, de, d), lambda i, te, tv: (layer, te[i], 0, 0))],
        out_specs=pl.BlockSpec((MOE_TM, d), lambda i, te, tv: (i, 0)),
        scratch_shapes=[pltpu.VMEM((d, de), BF16), pltpu.VMEM((d, de), BF16), pltpu.VMEM((de, d), BF16),
                        pltpu.VMEM((2 * MOE_TM * sub, 128), F32), pltpu.SemaphoreType.DMA((2,))],
    )
    return pl.pallas_call(
        _moe_kernel,
        out_shape=_sds((n_rows, d), BF16),
        grid_spec=grid_spec,
        compiler_params=_cparams(("arbitrary",)),
        name="moe_experts",
    )(tile_expert, tile_valid, toks, toks, x.reshape(t, sub, 128), w_gate, w_up, w_down)


def _top_k_desc(vals, k):
    n = vals.shape[-1]
    iota = lax.broadcasted_iota(jnp.int32, vals.shape, vals.ndim - 1)
    top_v, top_i = [], []
    for _ in range(k):
        m = jnp.max(vals, axis=-1, keepdims=True)
        idx = jnp.min(jnp.where(vals == m, iota, n), axis=-1, keepdims=True)
        top_v.append(m)
        top_i.append(idx)
        vals = jnp.where(iota == idx, -jnp.inf, vals)
    return jnp.concatenate(top_v, axis=-1), jnp.concatenate(top_i, axis=-1)


def _route(logits, e_bias):
    t = logits.shape[0]
    scores = jax.nn.sigmoid(logits)
    sel = scores + e_bias.astype(F32)
    per_group = sel.reshape(t, N_GROUPS, N_EXPERTS // N_GROUPS)
    group_score = _top_k_desc(per_group, 2)[0].sum(-1)
    _, g_idx = _top_k_desc(group_score, TOPK_GROUPS)
    g_mask = (g_idx[:, :, None] == jnp.arange(N_GROUPS)[None, None, :]).any(axis=1)
    e_mask = jnp.repeat(g_mask, N_EXPERTS // N_GROUPS, axis=1)
    _, e_idx = _top_k_desc(jnp.where(e_mask, sel, -jnp.inf), TOP_K)
    wts = jnp.take_along_axis(scores, e_idx, axis=1)
    wts = wts / wts.sum(-1, keepdims=True) * ROUTED_SCALE
    return e_idx, wts


def _running_count_kernel(m_ref, o_ref, carry_ref):
    i = pl.program_id(0)

    @pl.when(i == 0)
    def _():
        carry_ref[...] = jnp.zeros_like(carry_ref)

    tm = m_ref.shape[0]
    lower = (lax.broadcasted_iota(jnp.int32, (tm, tm), 0)
             >= lax.broadcasted_iota(jnp.int32, (tm, tm), 1)).astype(BF16)
    counts = _dot(lower, m_ref[...]) + carry_ref[...]
    o_ref[...] = counts
    carry_ref[...] = counts[tm - 1:tm, :]


def _running_count(member, tm):
    t, n = member.shape
    assert t % tm == 0
    return pl.pallas_call(
        _running_count_kernel,
        out_shape=_sds((t, n), F32),
        grid=(t // tm,),
        in_specs=[pl.BlockSpec((tm, n), lambda i: (i, 0))],
        out_specs=pl.BlockSpec((tm, n), lambda i: (i, 0)),
        scratch_shapes=[pltpu.VMEM((1, n), F32)],
        compiler_params=_cparams(("arbitrary",)),
        name="running_count",
    )(member)


def _dispatch_plan(e_idx):
    t = e_idx.shape[0]
    n_assign = t * TOP_K
    member = (e_idx[:, :, None] == jnp.arange(N_EXPERTS)[None, None, :]).any(axis=1)
    running = _running_count(member.astype(BF16), math.gcd(t, 832)).astype(jnp.int32)
    counts = running[-1]
    rank = jnp.take_along_axis(running, e_idx, axis=1) - 1
    padded = (counts + MOE_TM - 1) // MOE_TM * MOE_TM
    pad_end = jnp.cumsum(padded)
    pad_start = pad_end - padded
    dest_of_assign = (pad_start[e_idx] + rank).astype(jnp.int32).reshape(-1)
    n_tiles = (n_assign + N_EXPERTS * (MOE_TM - 1)) // MOE_TM + 1
    tok_of_assign = jnp.repeat(jnp.arange(t, dtype=jnp.int32), TOP_K)
    row_tok = jnp.zeros((n_tiles * MOE_TM,), jnp.int32).at[dest_of_assign].set(tok_of_assign)
    tile_start = jnp.arange(n_tiles) * MOE_TM
    tile_expert = jnp.minimum((tile_start[:, None] >= pad_end[None, :]).sum(axis=1),
                              N_EXPERTS - 1).astype(jnp.int32)
    tile_valid = (tile_start < pad_end[-1]).astype(jnp.int32)
    last_used = tile_expert[jnp.maximum(pad_end[-1] // MOE_TM - 1, 0)]
    tile_expert = jnp.where(tile_valid > 0, tile_expert, last_used)
    return row_tok, tile_expert, tile_valid, dest_of_assign


def _final_kernel(xf_ref, xb_ref, y_ref, wt_ref, p_ref, wsg_ref, wsu_ref, wsd_ref, wpg_ref, wpp_ref,
                  g_ref, b_ref, of_ref, ob_ref):
    wt = wt_ref[...].astype(BF16).astype(F32)
    routed = wt[:, 0:1] * y_ref[0].astype(F32)
    for j in range(1, wt.shape[1]):
        routed = routed + wt[:, j:j + 1] * y_ref[j].astype(F32)
    x = xb_ref[...]
    a = _dot(x, wsg_ref[...])
    u = _dot(x, wsu_ref[...])
    shared = _dot((_silu(a) * u).astype(BF16), wsd_ref[...])
    gate = jax.nn.sigmoid(_dot(x, wpg_ref[...]))
    ple = gate * _dot(p_ref[...], wpp_ref[...])
    v = ALPHA * xf_ref[...] + (routed + shared) + ple
    y = _layer_norm_rows(v, g_ref[...], b_ref[...])
    of_ref[...] = y
    ob_ref[...] = y.astype(BF16)


def _final_block(xf, xb, y_assign, wts, p, wsg, wsu, wsd, wpg, wpp, g, b, tm):
    m, d = xf.shape
    de = wsg.shape[1]
    dp = p.shape[1]
    nk = wts.shape[1]
    assert m % tm == 0
    row = lambda i: (i, 0)
    const = lambda i: (0, 0)
    once = dict(pipeline_mode=pl.Buffered(1))
    return pl.pallas_call(
        _final_kernel,
        out_shape=(_sds((m, d), F32), _sds((m, d), BF16)),
        grid=(m // tm,),
        in_specs=[pl.BlockSpec((tm, d), row), pl.BlockSpec((tm, d), row),
                  pl.BlockSpec((nk, tm, d), lambda i: (0, i, 0)), pl.BlockSpec((tm, nk), row),
                  pl.BlockSpec((tm, dp), row),
                  pl.BlockSpec((d, de), const, **once), pl.BlockSpec((d, de), const, **once),
                  pl.BlockSpec((de, d), const, **once), pl.BlockSpec((d, d), const, **once),
                  pl.BlockSpec((dp, d), const, **once),
                  pl.BlockSpec((1, d), const), pl.BlockSpec((1, d), const)],
        out_specs=(pl.BlockSpec((tm, d), row), pl.BlockSpec((tm, d), row)),
        compiler_params=_cparams(("parallel",)),
        name="shared_ple_ln",
    )(xf, xb, y_assign, wts, p, wsg, wsu, wsd, wpg, wpp, g.reshape(1, d), b.reshape(1, d))


def _attn_prompt_kernel(slopes_ref, q_ref, k_ref, v_ref, o_ref, kk, vv, m_s, l_s, acc_s):
    h = pl.program_id(1)
    sl = pl.program_id(2)

    @pl.when(sl == 0)
    def _():
        kk[0:SLAB, :] = jnp.zeros((SLAB, HEAD_DIM), F32)
        vv[0:SLAB, :] = jnp.zeros((SLAB, HEAD_DIM), F32)

    kk[SLAB:2 * SLAB, :] = k_ref[...]
    vv[SLAB:2 * SLAB, :] = v_ref[...]

    slope = slopes_ref[h]
    ii = lax.broadcasted_iota(jnp.int32, (BAND, BAND), 0)
    jj = lax.broadcasted_iota(jnp.int32, (BAND, BAND), 1)
    dist_prev = (ii + BAND - jj).astype(F32)
    dist_cur = (ii - jj).astype(F32)

    for di, d in enumerate(DILATIONS):
        c = slope * float(d)
        bias_prev = jnp.where(jj >= ii, -c * dist_prev, NEG)
        bias_cur = jnp.where(jj <= ii, -c * dist_cur, NEG)

        def units(it, carry, di=di, d=d, bias_prev=bias_prev, bias_cur=bias_cur):
            def rows(start):
                if d == 1:
                    return pl.ds(start, BAND)
                return pl.ds(start, BAND, stride=d)

            us = [it * ATT_UNROLL + j for j in range(ATT_UNROLL)]
            sp = [u // d for u in us]
            base = [s * (BAND * d) + (u - s * d) for u, s in zip(us, sp)]
            n = range(ATT_UNROLL)
            q = [q_ref[rows(base[j]), :].astype(BF16) for j in n]
            kc = [kk[rows(base[j] + SLAB), :].astype(BF16) for j in n]
            kp = [kk[rows(base[j] + SLAB - BAND * d), :].astype(BF16) for j in n]
            pen = [jnp.where(jnp.logical_and(sl == 0, sp[j] == 0), NEG, 0.0) for j in n]
            s_p = [_dot_nt(q[j], kp[j]) * ATT_SCALE + (bias_prev + pen[j]) for j in n]
            s_c = [_dot_nt(q[j], kc[j]) * ATT_SCALE + bias_cur for j in n]
            m = [jnp.maximum(jnp.max(s_p[j], axis=-1, keepdims=True), jnp.max(s_c[j], axis=-1, keepdims=True))
                 for j in n]
            p_p = [jnp.exp(s_p[j] - m[j]) for j in n]
            p_c = [jnp.exp(s_c[j] - m[j]) for j in n]
            l = [jnp.sum(p_p[j], axis=-1, keepdims=True) + jnp.sum(p_c[j], axis=-1, keepdims=True) for j in n]
            vc = [vv[rows(base[j] + SLAB), :].astype(BF16) for j in n]
            vp = [vv[rows(base[j] + SLAB - BAND * d), :].astype(BF16) for j in n]
            acc = [_dot(p_p[j].astype(BF16), vp[j]) + _dot(p_c[j].astype(BF16), vc[j]) for j in n]
            for j in n:
                m_s[di, rows(base[j]), :] = jnp.broadcast_to(m[j], (BAND, HEAD_DIM))
                l_s[di, rows(base[j]), :] = jnp.broadcast_to(l[j], (BAND, HEAD_DIM))
                acc_s[di, rows(base[j]), :] = acc[j]
            return carry

        lax.fori_loop(0, SLAB // BAND // ATT_UNROLL, units, 0)

    def merge(t, carry):
        rows = pl.ds(pl.multiple_of(t * BAND, BAND), BAND)
        m_max = jnp.maximum(jnp.maximum(m_s[0, rows, :], m_s[1, rows, :]), m_s[2, rows, :])
        w0 = jnp.exp(m_s[0, rows, :] - m_max)
        num = w0 * acc_s[0, rows, :]
        den = w0 * l_s[0, rows, :]
        for di in range(1, len(DILATIONS)):
            w = jnp.exp(m_s[di, rows, :] - m_max)
            num = num + w * acc_s[di, rows, :]
            den = den + w * l_s[di, rows, :]
        o_ref[rows, :] = (num / den).astype(o_ref.dtype)
        return carry

    lax.fori_loop(0, SLAB // BAND, merge, 0)
    kk[0:SLAB, :] = kk[SLAB:2 * SLAB, :]
    vv[0:SLAB, :] = vv[SLAB:2 * SLAB, :]


def _attn_prompt(proj, slopes, n_batch, seq):
    assert seq % SLAB == 0
    ns = seq // SLAB
    grid_spec = pltpu.PrefetchScalarGridSpec(
        num_scalar_prefetch=1,
        grid=(n_batch, ATT_HEADS, ns),
        in_specs=[pl.BlockSpec((SLAB, HEAD_DIM), lambda b, h, s, sl: (b * ns + s, h)),
                  pl.BlockSpec((SLAB, HEAD_DIM), lambda b, h, s, sl: (b * ns + s, ATT_HEADS + h)),
                  pl.BlockSpec((SLAB, HEAD_DIM), lambda b, h, s, sl: (b * ns + s, 2 * ATT_HEADS + h))],
        out_specs=pl.BlockSpec((SLAB, HEAD_DIM), lambda b, h, s, sl: (b * ns + s, h)),
        scratch_shapes=[pltpu.VMEM((2 * SLAB, HEAD_DIM), F32), pltpu.VMEM((2 * SLAB, HEAD_DIM), F32),
                        pltpu.VMEM((len(DILATIONS), SLAB, HEAD_DIM), F32),
                        pltpu.VMEM((len(DILATIONS), SLAB, HEAD_DIM), F32),
                        pltpu.VMEM((len(DILATIONS), SLAB, HEAD_DIM), F32)],
    )
    return pl.pallas_call(
        _attn_prompt_kernel,
        out_shape=_sds((n_batch * seq, ATT_W), BF16),
        grid_spec=grid_spec,
        compiler_params=_cparams(("parallel", "parallel", "arbitrary")),
        name="attn_prompt",
    )(slopes, proj, proj, proj)


def _attn_sample_kernel(slopes_ref, q_ref, kn_ref, vn_ref, k_ref, v_ref, o_ref):
    hb = pl.program_id(1)
    n_heads, n_buf = k_ref.shape[0], k_ref.shape[1]
    steps = (BAND - lax.broadcasted_iota(jnp.int32, (BAND, 1), 0)).astype(F32)
    for hh in range(n_heads):
        h = hb * n_heads + hh
        head = pl.ds(h, 1)
        q = q_ref[head, :]
        v_self = vn_ref[head, :]
        slope = slopes_ref[h]
        s_self = jnp.sum(q * kn_ref[head, :], axis=-1, keepdims=True) * ATT_SCALE
        parts = []
        for d in DILATIONS:
            start = n_buf - BAND * d
            rows = pl.ds(start, BAND) if d == 1 else pl.ds(start, BAND, stride=d)
            sc = (jnp.sum(k_ref[hh, rows, :] * q, axis=-1, keepdims=True) * ATT_SCALE
                  - slope * (steps * float(d)))
            m_d = jnp.maximum(jnp.max(sc, axis=0, keepdims=True), s_self)
            p = jnp.exp(sc - m_d)
            p_self = jnp.exp(s_self - m_d)
            l_d = jnp.sum(p, axis=0, keepdims=True) + p_self
            acc = jnp.sum(p * v_ref[hh, rows, :], axis=0, keepdims=True) + p_self * v_self
            parts.append((m_d, l_d, acc))
        m_max = jnp.maximum(jnp.maximum(parts[0][0], parts[1][0]), parts[2][0])
        num = jnp.zeros(q.shape, F32)
        den = jnp.zeros(s_self.shape, F32)
        for m_d, l_d, acc in parts:
            w = jnp.exp(m_d - m_max)
            num = num + w * acc
            den = den + w * l_d
        o_ref[head, :] = num / den


def _attn_sample(q, k_new, v_new, k_buf, v_buf, slopes):
    nb, nh, n_buf, e = k_buf.shape
    hpb = math.gcd(nh, ATT_SAMPLE_HEADS)
    assert n_buf >= BAND * DILATIONS[-1]
    tok = pl.BlockSpec((None, nh, e), lambda b, h, sl: (b, 0, 0))
    buf = pl.BlockSpec((None, hpb, n_buf, e), lambda b, h, sl: (b, h, 0, 0))
    grid_spec = pltpu.PrefetchScalarGridSpec(
        num_scalar_prefetch=1, grid=(nb, nh // hpb), in_specs=[tok, tok, tok, buf, buf], out_specs=tok)
    return pl.pallas_call(
        _attn_sample_kernel,
        out_shape=_sds((nb, nh, e), F32),
        grid_spec=grid_spec,
        compiler_params=_cparams(("parallel", "arbitrary")),
        name="attn_sample",
    )(slopes, q, k_new, v_new, k_buf, v_buf)


def _shifted(u, prev8, shift):
    row = lax.broadcasted_iota(jnp.int32, u.shape, 0)
    out = pltpu.roll(u, shift, 0)
    for s in range(shift):
        out = jnp.where(row == s, prev8[8 - shift + s:8 - shift + s + 1, :], out)
    return out


def _sconv_prompt_kernel(h_ref, gb_ref, gc_ref, hp_ref, gcp_ref, w_ref, o_ref):
    i = pl.program_id(1)
    u = gc_ref[...] * h_ref[...]
    up = jnp.where(i == 0, 0.0, gcp_ref[...] * hp_ref[...])
    w = w_ref[...]
    conv = w[0:1] * _shifted(u, up, 2) + w[1:2] * _shifted(u, up, 1) + w[2:3] * u
    o_ref[...] = (gb_ref[...] * conv).astype(o_ref.dtype)


def _sconv_prompt(proj, w, n_batch, seq, tm):
    sc = w.shape[1]
    nt = seq // tm
    cb = 3 * ATT_W // sc
    cur = lambda c: pl.BlockSpec((tm, sc), lambda b, i: (b * nt + i, cb + c))
    prev = lambda c: pl.BlockSpec((8, sc), lambda b, i: (jnp.maximum((b * nt + i) * (tm // 8) - 1, 0), cb + c))
    return pl.pallas_call(
        _sconv_prompt_kernel,
        out_shape=_sds((n_batch * seq, sc), BF16),
        grid=(n_batch, nt),
        in_specs=[cur(0), cur(1), cur(2), prev(0), prev(2), pl.BlockSpec((SC_WIDTH, sc), lambda b, i: (0, 0))],
        out_specs=pl.BlockSpec((tm, sc), lambda b, i: (b * nt + i, 0)),
        compiler_params=_cparams(("parallel", "parallel")),
        name="sconv_prompt",
    )(proj, proj, proj, proj, proj, w)


def _sconv_sample_kernel(h_ref, gb_ref, gc_ref, p0_ref, p1_ref, w_ref, o_ref, u_ref):
    u = gc_ref[...] * h_ref[...]
    w = w_ref[...]
    conv = w[0:1] * p0_ref[...] + w[1:2] * p1_ref[...] + w[2:3] * u
    o_ref[...] = (gb_ref[...] * conv).astype(o_ref.dtype)
    u_ref[...] = u


def _sconv_sample(h, gb, gc, p0, p1, w):
    nb, sc = h.shape
    return pl.pallas_call(
        _sconv_sample_kernel,
        out_shape=(_sds((nb, sc), BF16), _sds((nb, sc), F32)),
        name="sconv_sample",
    )(h, gb, gc, p0, p1, w)


def _qk_normalise(y, cb, n_qk_blocks):
    scale = jnp.where(cb < n_qk_blocks // 2, DN_DIM ** -0.5, 1.0)
    segs = []
    for s in range(y.shape[1] // DN_DIM):
        seg = y[:, s * DN_DIM:(s + 1) * DN_DIM]
        ss = jnp.sum(seg * seg, axis=-1, keepdims=True)
        segs.append(seg * (lax.rsqrt(ss + L2_EPS) * scale))
    return jnp.concatenate(segs, axis=1)


def _dconv_prompt_kernel(x_ref, xp_ref, w_ref, o_ref, *, n_qk_blocks):
    cb = pl.program_id(0)
    i = pl.program_id(2)
    x = x_ref[...]
    xp = jnp.where(i == 0, 0.0, xp_ref[...])
    w = w_ref[...]
    conv = (w[0:1] * _shifted(x, xp, 3) + w[1:2] * _shifted(x, xp, 2)
            + w[2:3] * _shifted(x, xp, 1) + w[3:4] * x)
    y = _silu(conv)

    @pl.when(cb < n_qk_blocks)
    def _():
        o_ref[...] = _qk_normalise(y, cb, n_qk_blocks)

    @pl.when(cb >= n_qk_blocks)
    def _():
        o_ref[...] = y


def _dconv_prompt(proj, w, n_batch, seq, tm, tc):
    nt = seq // tm
    ncb = DN_CONV_DIM // tc
    return pl.pallas_call(
        functools.partial(_dconv_prompt_kernel, n_qk_blocks=2 * DN_QK_W // tc),
        out_shape=_sds((n_batch * seq, DN_CONV_DIM), F32),
        grid=(ncb, n_batch, nt),
        in_specs=[pl.BlockSpec((tm, tc), lambda c, b, i: (b * nt + i, c)),
                  pl.BlockSpec((8, tc), lambda c, b, i: (jnp.maximum((b * nt + i) * (tm // 8) - 1, 0), c)),
                  pl.BlockSpec((DN_CONV_WIDTH, tc), lambda c, b, i: (0, c))],
        out_specs=pl.BlockSpec((tm, tc), lambda c, b, i: (b * nt + i, c)),
        compiler_params=_cparams(("parallel", "parallel", "parallel")),
        name="dconv_prompt",
    )(proj, proj, w)


def _dconv_sample_kernel(x_ref, p0_ref, p1_ref, p2_ref, w_ref, o_ref, *, n_qk_blocks):
    cb = pl.program_id(0)
    w = w_ref[...]
    conv = w[0:1] * p0_ref[...] + w[1:2] * p1_ref[...] + w[2:3] * p2_ref[...] + w[3:4] * x_ref[...]
    y = _silu(conv)

    @pl.when(cb < n_qk_blocks)
    def _():
        o_ref[...] = _qk_normalise(y, cb, n_qk_blocks)

    @pl.when(cb >= n_qk_blocks)
    def _():
        o_ref[...] = y


def _dconv_sample(x, p0, p1, p2, w, tc):
    nb = x.shape[0]
    blk = pl.BlockSpec((nb, tc), lambda c: (0, c))
    return pl.pallas_call(
        functools.partial(_dconv_sample_kernel, n_qk_blocks=2 * DN_QK_W // tc),
        out_shape=_sds((nb, DN_CONV_DIM), F32),
        grid=(DN_CONV_DIM // tc,),
        in_specs=[blk, blk, blk, blk, pl.BlockSpec((DN_CONV_WIDTH, tc), lambda c: (0, c))],
        out_specs=blk,
        compiler_params=_cparams(("parallel",)),
        name="dconv_sample",
    )(x, p0, p1, p2, w)


def _gated_rms(o, z, norm_w):
    return o * lax.rsqrt(jnp.mean(o * o, axis=-1, keepdims=True) + RMS_EPS) * norm_w * _silu(z)


def _delta_prompt_kernel(q_ref, k_ref, v_ref, z_ref, g_ref, gt_ref, beta_ref, nw_ref, o_ref, s_out_ref, s_ref,
                         *, n_chunks):
    c = pl.program_id(2)

    @pl.when(c == 0)
    def _():
        s_ref[...] = jnp.zeros_like(s_ref)

    cc = DN_CHUNK
    heads = range(DN_GROUP_V)
    ii = lax.broadcasted_iota(jnp.int32, (cc, cc), 0)
    jj = lax.broadcasted_iota(jnp.int32, (cc, cc), 1)
    beta_all = beta_ref[...]
    norm_w = nw_ref[...]

    lower = (ii >= jj).astype(BF16)
    gc_all = _dot_exact_lhs(lower, g_ref[...])
    kk = lax.broadcasted_iota(jnp.int32, (2 * cc, cc), 0) - lax.rem(c, 2) * cc
    tt = lax.broadcasted_iota(jnp.int32, (2 * cc, cc), 1)
    upto = jnp.logical_and(kk >= 0, kk <= tt).astype(BF16)
    gr_all = _dot_exact_rhs(gt_ref[...], upto)

    q = [q_ref[:, h * DN_DIM:(h + 1) * DN_DIM] for h in range(DN_GROUP_QK)]
    k = [k_ref[:, h * DN_DIM:(h + 1) * DN_DIM] for h in range(DN_GROUP_QK)]
    k_b = [x.astype(BF16) for x in k]
    qk = [_dot_nt(q[h].astype(BF16), k_b[h]) for h in range(DN_GROUP_QK)]

    gc = [gc_all[:, h:h + 1] for h in heads]
    beta = [beta_all[:, h:h + 1] for h in heads]
    decay = [jnp.where(ii >= jj, jnp.exp(jnp.minimum(gc[h] - gr_all[h:h + 1, :], 0.0)), 0.0) for h in heads]
    eg = [jnp.exp(gc[h]) for h in heads]
    kbeta = [k[h // 2] * beta[h] for h in heads]
    npow = [-jnp.where(ii > jj, _dot_nt(kbeta[h].astype(BF16), k_b[h // 2]) * decay[h], 0.0) for h in heads]
    y = [jnp.concatenate([v_ref[:, h * DN_DIM:(h + 1) * DN_DIM] * beta[h], kbeta[h] * eg[h]], axis=1)
         for h in heads]
    for step in range(6):
        upd = [_dot3(npow[h], y[h]) for h in heads]
        y = [y[h] + upd[h] for h in heads]
        if step < 5:
            npow = [_dot3(npow[h], npow[h]) for h in heads]
    s_old = [s_ref[h] for h in heads]
    s_b = [x.astype(BF16) for x in s_old]
    v_new = [y[h][:, :DN_DIM] - _dot(y[h][:, DN_DIM:].astype(BF16), s_b[h]) for h in heads]
    v_new_b = [x.astype(BF16) for x in v_new]
    o = [_dot((q[h // 2] * eg[h]).astype(BF16), s_b[h]) + _dot((qk[h // 2] * decay[h]).astype(BF16), v_new_b[h])
         for h in heads]
    for h in heads:
        g_last = gc[h][cc - 1:cc, :]
        k_dec = k[h // 2] * jnp.exp(g_last - gc[h])
        s_ref[h] = s_old[h] * jnp.exp(g_last) + _dot_tn(k_dec.astype(BF16), v_new_b[h])
    for h in heads:
        z = z_ref[:, h * DN_DIM:(h + 1) * DN_DIM]
        o_ref[:, h * DN_DIM:(h + 1) * DN_DIM] = _gated_rms(o[h], z, norm_w).astype(o_ref.dtype)

    @pl.when(c == n_chunks - 1)
    def _():
        s_out_ref[...] = s_ref[...]


def _delta_prompt(qkv, proj, g, g_t, beta, norm_w, n_batch, seq):
    assert seq % (2 * DN_CHUNK) == 0
    nc = seq // DN_CHUNK
    n_hg = DN_QK_HEADS // DN_GROUP_QK
    wq = DN_GROUP_QK * DN_DIM
    wv = DN_GROUP_V * DN_DIM
    row = lambda b, hg, c: b * nc + c
    return pl.pallas_call(
        functools.partial(_delta_prompt_kernel, n_chunks=nc),
        out_shape=(_sds((n_batch * seq, DN_VW), BF16), _sds((n_batch, DN_V_HEADS, DN_DIM, DN_DIM), F32)),
        grid=(n_batch, n_hg, nc),
        in_specs=[pl.BlockSpec((DN_CHUNK, wq), lambda b, hg, c: (row(b, hg, c), hg)),
                  pl.BlockSpec((DN_CHUNK, wq), lambda b, hg, c: (row(b, hg, c), DN_QK_W // wq + hg)),
                  pl.BlockSpec((DN_CHUNK, wv), lambda b, hg, c: (row(b, hg, c), 2 * DN_QK_W // wv + hg)),
                  pl.BlockSpec((DN_CHUNK, wv), lambda b, hg, c: (row(b, hg, c), DN_CONV_DIM // wv + hg)),
                  pl.BlockSpec((None, DN_CHUNK, DN_GROUP_V), lambda b, hg, c: (hg, row(b, hg, c), 0)),
                  pl.BlockSpec((None, DN_GROUP_V, 2 * DN_CHUNK), lambda b, hg, c: (hg, 0, row(b, hg, c) // 2)),
                  pl.BlockSpec((None, DN_CHUNK, DN_GROUP_V), lambda b, hg, c: (hg, row(b, hg, c), 0)),
                  pl.BlockSpec((1, DN_DIM), lambda b, hg, c: (0, 0))],
        out_specs=(pl.BlockSpec((DN_CHUNK, wv), lambda b, hg, c: (row(b, hg, c), hg)),
                   pl.BlockSpec((None, DN_GROUP_V, DN_DIM, DN_DIM), lambda b, hg, c: (b, hg, 0, 0))),
        scratch_shapes=[pltpu.VMEM((DN_GROUP_V, DN_DIM, DN_DIM), F32)],
        compiler_params=_cparams(("parallel", "parallel", "arbitrary")),
        name="delta_prompt",
    )(qkv, qkv, qkv, proj, g, g_t, beta, norm_w.reshape(1, DN_DIM))


def _delta_sample_kernel(qt_ref, kt_ref, v_ref, z_ref, g_ref, beta_ref, nw_ref, s_ref, o_ref, s_out_ref):
    norm_w = nw_ref[...]
    qt = qt_ref[...]
    kt = kt_ref[...]
    for hv in range(DN_V_HEADS):
        hq = hv // 2
        k_col = jnp.broadcast_to(kt[:, hq:hq + 1], (DN_DIM, DN_DIM))
        q_col = jnp.broadcast_to(qt[:, hq:hq + 1], (DN_DIM, DN_DIM))
        eg = jnp.exp(g_ref[hv:hv + 1, :])
        beta = beta_ref[hv:hv + 1, :]
        s = s_ref[hv]
        sk = jnp.sum(s * k_col, axis=0, keepdims=True)
        delta = beta * (v_ref[hv:hv + 1, :] - eg * sk)
        s_new = s * eg + k_col * delta
        s_out_ref[hv] = s_new
        o = jnp.sum(s_new * q_col, axis=0, keepdims=True)
        o_ref[hv:hv + 1, :] = _gated_rms(o, z_ref[hv:hv + 1, :], norm_w).astype(o_ref.dtype)


def _delta_sample(qt, kt, v, z, g_b, beta_b, norm_w, s0):
    nb = v.shape[0]
    per_b3 = lambda shape: pl.BlockSpec((None,) + shape, lambda b: (b, 0, 0))
    st = pl.BlockSpec((None, DN_V_HEADS, DN_DIM, DN_DIM), lambda b: (b, 0, 0, 0))
    return pl.pallas_call(
        _delta_sample_kernel,
        out_shape=(_sds((nb, DN_V_HEADS, DN_DIM), BF16), _sds(s0.shape, F32)),
        grid=(nb,),
        in_specs=[per_b3((DN_DIM, DN_QK_HEADS)), per_b3((DN_DIM, DN_QK_HEADS)),
                  per_b3((DN_V_HEADS, DN_DIM)), per_b3((DN_V_HEADS, DN_DIM)),
                  per_b3((DN_V_HEADS, DN_DIM)), per_b3((DN_V_HEADS, DN_DIM)),
                  pl.BlockSpec((1, DN_DIM), lambda b: (0, 0)), st],
        out_specs=(per_b3((DN_V_HEADS, DN_DIM)), st),
        compiler_params=_cparams(("parallel",)),
        name="delta_sample",
    )(qt, kt, v, z, g_b, beta_b, norm_w.reshape(1, DN_DIM), s0)


def _mixer_layer(xf, xb, w_in, sconv_w, w_out, k_buf, v_buf, sconv_prev, n_batch, seq, g1, b1):
    tp = n_batch * seq
    proj = _matmul(xb, w_in.astype(BF16), 832, 512)
    slopes = jnp.exp2(-8.0 * (jnp.arange(ATT_HEADS, dtype=F32) + 1.0) / ATT_HEADS)
    att_p = _attn_prompt(proj, slopes, n_batch, seq)
    ns = xf.shape[0] - tp
    ps = proj[tp:]
    q_s, k_s, v_s = (ps[:, i * ATT_W:(i + 1) * ATT_W] for i in range(3))
    heads = lambda a: a.reshape(ns, ATT_HEADS, HEAD_DIM)
    att_s = _attn_sample(heads(q_s), heads(k_s), heads(v_s), k_buf.transpose(0, 2, 1, 3),
                         v_buf.transpose(0, 2, 1, 3), slopes).astype(BF16)
    sc = sconv_w.shape[1]
    gc_p = _sconv_prompt(proj, sconv_w, n_batch, seq, 512)
    h_s, gb_s, gcs_s = (ps[:, 3 * ATT_W + i * sc:3 * ATT_W + (i + 1) * sc] for i in range(3))
    gc_s, u_s = _sconv_sample(h_s, gb_s, gcs_s, sconv_prev[:, 0], sconv_prev[:, 1], sconv_w)
    mixed = jnp.concatenate([jnp.concatenate([att_p, att_s.reshape(ns, ATT_W)], axis=0),
                             jnp.concatenate([gc_p, gc_s], axis=0)], axis=1)
    x1f, x1b = _matmul_residual_ln(mixed, w_out.astype(BF16), xf, g1, b1, 320, w_out.shape[0])

    keep = min(SLAB, seq)
    tail = lambda n, c0, c1: jnp.stack([proj[(b + 1) * seq - n:(b + 1) * seq, c0:c1] for b in range(n_batch)])
    k_rows_p = tail(keep, ATT_W, 2 * ATT_W).reshape(n_batch, keep, ATT_HEADS, HEAD_DIM)
    v_rows_p = tail(keep, 2 * ATT_W, 3 * ATT_W).reshape(n_batch, keep, ATT_HEADS, HEAD_DIM)
    u_tail = (tail(SC_WIDTH - 1, 3 * ATT_W + 2 * sc, 3 * ATT_W + 3 * sc)
              * tail(SC_WIDTH - 1, 3 * ATT_W, 3 * ATT_W + sc))
    sconv_new_s = jnp.stack([sconv_prev[:, 1], u_s], axis=1)
    outs_p = (k_rows_p, v_rows_p, u_tail)
    outs_s = (k_s.reshape(ns, 1, ATT_HEADS, HEAD_DIM), v_s.reshape(ns, 1, ATT_HEADS, HEAD_DIM), sconv_new_s)
    return x1f, x1b, outs_p, outs_s


def _delta_layer(xf, xb, w_in, conv_w, a_log, dt_bias, norm_w, w_out, conv_prev, s0, n_batch, seq, g1, b1):
    tp = n_batch * seq
    ns = xf.shape[0] - tp
    n_main = DN_CONV_DIM + DN_VW
    proj = _matmul(xb, w_in[:, :n_main].astype(BF16), 832, 512)
    ba = _matmul(xb, w_in[:, n_main:].astype(BF16), 832, 2 * DN_V_HEADS)
    beta = jax.nn.sigmoid(ba[:, :DN_V_HEADS])
    g = -jnp.exp(a_log.astype(F32)) * jax.nn.softplus(ba[:, DN_V_HEADS:] + dt_bias.astype(F32))

    qkv_p = _dconv_prompt(proj, conv_w, n_batch, seq, 512, 512)
    n_vg = DN_V_HEADS // DN_GROUP_V
    grp = lambda a: a[:tp].reshape(tp, n_vg, DN_GROUP_V).transpose(1, 0, 2)
    o_p, s_p = _delta_prompt(qkv_p, proj, grp(g), grp(g).transpose(0, 2, 1), grp(beta), norm_w, n_batch, seq)

    ps = proj[tp:]
    qkv_s = _dconv_sample(ps[:, :DN_CONV_DIM], conv_prev[:, 0], conv_prev[:, 1], conv_prev[:, 2], conv_w, 512)
    qt = qkv_s[:, :DN_QK_W].reshape(ns, DN_QK_HEADS, DN_DIM).transpose(0, 2, 1)
    kt = qkv_s[:, DN_QK_W:2 * DN_QK_W].reshape(ns, DN_QK_HEADS, DN_DIM).transpose(0, 2, 1)
    v_s = qkv_s[:, 2 * DN_QK_W:].reshape(ns, DN_V_HEADS, DN_DIM)
    z_s = ps[:, DN_CONV_DIM:].reshape(ns, DN_V_HEADS, DN_DIM)
    lanes = lambda a: jnp.broadcast_to(a[tp:, :, None], (ns, DN_V_HEADS, DN_DIM))
    o_s, s_s = _delta_sample(qt, kt, v_s, z_s, lanes(g), lanes(beta), norm_w, s0)

    og = jnp.concatenate([o_p, o_s.reshape(ns, DN_VW)], axis=0)
    x1f, x1b = _matmul_residual_ln(og, w_out.astype(BF16), xf, g1, b1, 320, 2048)

    dconv_new_p = jnp.stack([proj[(b + 1) * seq - (DN_CONV_WIDTH - 1):(b + 1) * seq, :DN_CONV_DIM]
                             for b in range(n_batch)])
    dconv_new_s = jnp.concatenate([conv_prev[:, 1:], ps[:, None, :DN_CONV_DIM]], axis=1)
    return x1f, x1b, (dconv_new_p, s_p), (dconv_new_s, s_s)


def _moe_block(x1f, x1b, p_b, w_router, e_bias, w_gate, w_up, w_down, layer, ws_gate, ws_up, ws_down,
               ple_w_proj, ple_w_gate, g2, b2):
    t, d = x1f.shape
    logits = _matmul(x1b, w_router.astype(BF16), 832, N_EXPERTS)
    e_idx, wts = _route(logits, e_bias)
    row_tok, tile_expert, tile_valid, dest_of_assign = _dispatch_plan(e_idx)
    y_rows = _moe_experts(x1f, row_tok, tile_expert, tile_valid, w_gate, w_up, w_down, layer)
    y_assign = y_rows[dest_of_assign.reshape(t, TOP_K).T.reshape(-1)].reshape(TOP_K, t, d)
    return _final_block(x1f, x1b, y_assign, wts, p_b, ws_gate.astype(BF16), ws_up.astype(BF16),
                        ws_down.astype(BF16), ple_w_gate.astype(BF16), ple_w_proj.astype(BF16), g2, b2, 160)


def kernel(x_prompt, x_sample, cache_attn_k, cache_attn_v, state_sconv, state_dconv, state_delta,
           p_prompt, p_sample, mix_w_in, mix_sconv_w, mix_w_out, dn_w_in, dn_conv_w, dn_a_log,
           dn_dt_bias, dn_norm_w, dn_w_out, ln1_g, ln1_b, ln2_g, ln2_b, moe_w_router, moe_e_bias,
           moe_w_gate, moe_w_up, moe_w_down, moe_ws_gate, moe_ws_up, moe_ws_down, ple_w_proj,
           ple_w_gate):
    n_batch, seq, d = x_prompt.shape
    ns = x_sample.shape[0]
    assert x_sample.shape[1] == 1
    tp = n_batch * seq
    xf = jnp.concatenate([x_prompt.reshape(tp, d), x_sample.reshape(ns, d)], axis=0)
    xb = xf.astype(BF16)
    p_all = jnp.concatenate([p_prompt.reshape(DEPTH, tp, -1), p_sample.reshape(DEPTH, ns, -1)],
                            axis=1).astype(BF16)

    mix_p, mix_s, dn_p, dn_s = [], [], [], []
    for i in range(DEPTH):
        j = i // 2
        if i % 2 == 0:
            x1f, x1b, o_p, o_s = _mixer_layer(
                xf, xb, mix_w_in[j], mix_sconv_w[j], mix_w_out[j], cache_attn_k[j], cache_attn_v[j],
                state_sconv[j], n_batch, seq, ln1_g[i], ln1_b[i])
            mix_p.append(o_p)
            mix_s.append(o_s)
        else:
            x1f, x1b, o_p, o_s = _delta_layer(
                xf, xb, dn_w_in[j], dn_conv_w[j], dn_a_log[j], dn_dt_bias[j], dn_norm_w[j], dn_w_out[j],
                state_dconv[j], state_delta[j], n_batch, seq, ln1_g[i], ln1_b[i])
            dn_p.append(o_p)
            dn_s.append(o_s)
        xf, xb = _moe_block(x1f, x1b, p_all[i], moe_w_router[i], moe_e_bias[i], moe_w_gate,
                            moe_w_up, moe_w_down, i, moe_ws_gate[i], moe_ws_up[i], moe_ws_down[i],
                            ple_w_proj[i], ple_w_gate[i], ln2_g[i], ln2_b[i])

    stack = lambda parts, idx: jnp.stack([p[idx] for p in parts])
    y_prompt = xf[:tp].reshape(n_batch, seq, d)
    y_sample = xf[tp:].reshape(ns, 1, d)
    return (y_prompt, y_sample,
            stack(mix_p, 0), stack(mix_p, 1), stack(mix_p, 2), stack(dn_p, 0), stack(dn_p, 1),
            stack(mix_s, 0), stack(mix_s, 1), stack(mix_s, 2), stack(dn_s, 0), stack(dn_s, 1))
```

```python
import functools
import math

import jax
import jax.numpy as jnp
from jax import lax
from jax.experimental import pallas as pl
from jax.experimental.pallas import tpu as pltpu

F32 = jnp.float32
BF16 = jnp.bfloat16

DEPTH = 2
HEAD_DIM = 128
ATT_HEADS = 12
ATT_W = ATT_HEADS * HEAD_DIM
SC_WIDTH = 3
DILATIONS = (1, 4, 16)
BAND = 128
SLAB = BAND * DILATIONS[-1]
ATT_SCALE = HEAD_DIM ** -0.5
ATT_UNROLL = 4
ATT_SAMPLE_HEADS = 4
DN_QK_HEADS = 16
DN_V_HEADS = 32
DN_DIM = 128
DN_QK_W = DN_QK_HEADS * DN_DIM
DN_VW = DN_V_HEADS * DN_DIM
DN_CONV_DIM = 2 * DN_QK_W + DN_VW
DN_CONV_WIDTH = 4
DN_CHUNK = 64
DN_GROUP_QK = 4
DN_GROUP_V = 2 * DN_GROUP_QK
N_EXPERTS = 64
N_GROUPS = 8
TOPK_GROUPS = 4
TOP_K = 8
ROUTED_SCALE = 2.5
ALPHA = (2 * DEPTH) ** 0.25
LN_EPS = 1e-5
RMS_EPS = 1e-6
L2_EPS = 1e-6
NEG = -1e30

MOE_TM = 256
VMEM_LIMIT = 56 * 1024 * 1024


def _sds(shape, dtype):
    return jax.ShapeDtypeStruct(shape, dtype)


def _cparams(sem):
    return pltpu.CompilerParams(dimension_semantics=sem, vmem_limit_bytes=VMEM_LIMIT)


def _silu(x):
    return x * jax.nn.sigmoid(x)


def _layer_norm_rows(v, g, b):
    mu = jnp.mean(v, axis=-1, keepdims=True)
    c = v - mu
    var = jnp.mean(c * c, axis=-1, keepdims=True)
    return c * lax.rsqrt(var + LN_EPS) * g + b


def _split3(x):
    hi = x.astype(BF16)
    r1 = x - hi.astype(F32)
    mid = r1.astype(BF16)
    lo = (r1 - mid.astype(F32)).astype(BF16)
    return hi, mid, lo


def _dot(a, b):
    return jnp.dot(a, b, preferred_element_type=F32)


def _dot_nt(a, b):
    return lax.dot_general(a, b, (((1,), (1,)), ((), ())), preferred_element_type=F32)


def _dot_tn(a, b):
    return lax.dot_general(a, b, (((0,), (0,)), ((), ())), preferred_element_type=F32)


def _dot_exact_lhs(sel_bf16, x):
    hi, mid, lo = _split3(x)
    return _dot(sel_bf16, hi) + _dot(sel_bf16, mid) + _dot(sel_bf16, lo)


def _dot_exact_rhs(x, sel_bf16):
    hi, mid, lo = _split3(x)
    return _dot(hi, sel_bf16) + _dot(mid, sel_bf16) + _dot(lo, sel_bf16)


def _dot3(a, b):
    a_hi = a.astype(BF16)
    a_lo = (a - a_hi.astype(F32)).astype(BF16)
    b_hi = b.astype(BF16)
    b_lo = (b - b_hi.astype(F32)).astype(BF16)
    return _dot(a_hi, b_hi) + _dot(a_hi, b_lo) + _dot(a_lo, b_hi)


def _mm_kernel(x_ref, w_ref, o_ref):
    o_ref[...] = _dot(x_ref[...], w_ref[...])


def _matmul(x, w, tm, tn):
    m, k = x.shape
    n = w.shape[1]
    assert m % tm == 0 and n % tn == 0
    return pl.pallas_call(
        _mm_kernel,
        out_shape=_sds((m, n), F32),
        grid=(m // tm, n // tn),
        in_specs=[pl.BlockSpec((tm, k), lambda i, j: (i, 0)),
                  pl.BlockSpec((k, tn), lambda i, j: (0, j))],
        out_specs=pl.BlockSpec((tm, tn), lambda i, j: (i, j)),
        compiler_params=_cparams(("parallel", "parallel")),
        name="matmul",
    )(x, w)


PACK_SUB = 8


def _pack_rows(y):
    half = y.shape[1] // 2
    yb = y.astype(BF16).astype(F32)
    lo = lax.shift_right_logical(pltpu.bitcast(yb[:, :half], jnp.uint32), jnp.uint32(16))
    hi = pltpu.bitcast(yb[:, half:], jnp.uint32) & jnp.uint32(0xFFFF0000)
    return lo | hi


def _unpack_rows(parts):
    lo = [pltpu.bitcast(lax.shift_left(w, jnp.uint32(16)), F32) for w in parts]
    hi = [pltpu.bitcast(w & jnp.uint32(0xFFFF0000), F32) for w in parts]
    return jnp.concatenate(lo + hi, axis=1).astype(BF16)


def _mm_ln_kernel(a_ref, w_ref, x_ref, g_ref, b_ref, of_ref, ob_ref, op_ref, acc_ref, *, nk):
    kk = pl.program_id(1)
    part = _dot(a_ref[...], w_ref[...])

    @pl.when(kk == 0)
    def _():
        acc_ref[...] = part

    @pl.when(kk > 0)
    def _():
        acc_ref[...] += part

    @pl.when(kk == nk - 1)
    def _():
        y = _layer_norm_rows(ALPHA * x_ref[...] + acc_ref[...], g_ref[...], b_ref[...])
        of_ref[...] = y
        ob_ref[...] = y.astype(BF16)
        packed = _pack_rows(y)
        tm = y.shape[0]
        for s in range(PACK_SUB):
            op_ref[pl.ds(s, tm, stride=PACK_SUB), :] = packed[:, s * 128:(s + 1) * 128]


def _matmul_residual_ln(a, w, x, g, b, tm, tk):
    m, k = a.shape
    d = w.shape[1]
    assert m % tm == 0 and k % tk == 0 and d == 2 * PACK_SUB * 128
    nk = k // tk
    return pl.pallas_call(
        functools.partial(_mm_ln_kernel, nk=nk),
        out_shape=(_sds((m, d), F32), _sds((m, d), BF16), _sds((m * PACK_SUB, 128), jnp.uint32)),
        grid=(m // tm, nk),
        in_specs=[pl.BlockSpec((tm, tk), lambda i, kk: (i, kk)),
                  pl.BlockSpec((tk, d), lambda i, kk: (kk, 0)),
                  pl.BlockSpec((tm, d), lambda i, kk: (i, 0)),
                  pl.BlockSpec((1, d), lambda i, kk: (0, 0)),
                  pl.BlockSpec((1, d), lambda i, kk: (0, 0))],
        out_specs=(pl.BlockSpec((tm, d), lambda i, kk: (i, 0)),
                   pl.BlockSpec((tm, d), lambda i, kk: (i, 0)),
                   pl.BlockSpec((tm * PACK_SUB, 128), lambda i, kk: (i, 0))),
        scratch_shapes=[pltpu.VMEM((tm, d), F32)],
        compiler_params=_cparams(("parallel", "arbitrary")),
        name="matmul_residual_ln",
    )(a, w, x, g.reshape(1, d), b.reshape(1, d))


def _moe_kernel(te_ref, tv_ref, tok_ref, tok_next_ref, x_hbm, wg_ref, wu_ref, wd_ref, o_ref,
                wg_s, wu_s, wd_s, xbuf, sems):
    i = pl.program_id(0)
    n = pl.num_programs(0)
    slot = lax.rem(i, 2)
    sub = x_hbm.shape[1]
    tm = o_ref.shape[0]

    def rows_of(sl, r):
        return xbuf.at[pl.ds(pl.multiple_of((sl * tm + r) * sub, sub), sub), :]

    def wait_gather(sl):
        whole = xbuf.at[pl.ds(pl.multiple_of(sl * tm * sub, tm * sub), tm * sub), :]
        pltpu.make_async_copy(whole, whole, sems.at[sl]).wait()

    @pl.when(jnp.logical_and(i == 0, tv_ref[0] > 0))
    def _():
        def body(r, carry):
            pltpu.make_async_copy(x_hbm.at[tok_ref[0, r]], rows_of(0, r), sems.at[0]).start()
            return carry
        lax.fori_loop(0, tm, body, 0, unroll=8)

    e = te_ref[i]
    prev = te_ref[jnp.maximum(i - 1, 0)]

    @pl.when(jnp.logical_or(i == 0, e != prev))
    def _():
        wg_s[...] = wg_ref[...].astype(BF16)
        wu_s[...] = wu_ref[...].astype(BF16)
        wd_s[...] = wd_ref[...].astype(BF16)

    @pl.when(tv_ref[i] > 0)
    def _():
        wait_gather(slot)
        for r in range(tm):
            pltpu.make_async_copy(x_hbm.at[tok_next_ref[0, r]], rows_of(1 - slot, r), sems.at[1 - slot]).start()
        base = slot * tm * sub
        x = _unpack_rows([xbuf[pl.ds(base + s, tm, stride=sub), :] for s in range(sub)])
        a = _dot(x, wg_s[...])
        u = _dot(x, wu_s[...])
        o_ref[...] = _dot((_silu(a) * u).astype(BF16), wd_s[...]).astype(o_ref.dtype)

    @pl.when(tv_ref[i] == 0)
    def _():
        o_ref[...] = jnp.zeros_like(o_ref)

    @pl.when(jnp.logical_and(tv_ref[i] == 0, tv_ref[jnp.maximum(i - 1, 0)] > 0))
    def _():
        wait_gather(slot)


def _moe_experts(x_packed, row_tok, tile_expert, tile_valid, w_gate, w_up, w_down, layer):
    sub = PACK_SUB
    t = x_packed.shape[0] // sub
    d = 2 * sub * 128
    n_rows = row_tok.shape[0]
    n_tiles = n_rows // MOE_TM
    de = w_gate.shape[-1]
    toks = row_tok.reshape(n_tiles, 1, MOE_TM)
    tok_spec = lambda off: pl.BlockSpec((None, 1, MOE_TM),
                                        lambda i, te, tv: (jnp.minimum(i + off, n_tiles - 1), 0, 0),
                                        memory_space=pltpu.SMEM)
    grid_spec = pltpu.PrefetchScalarGridSpec(
        num_scalar_prefetch=2,
        grid=(n_tiles,),
        in_specs=[tok_spec(0), tok_spec(1), pl.BlockSpec(memory_space=pl.ANY),
                  pl.BlockSpec((None, None, d, de), lambda i, te, tv: (layer, te[i], 0, 0)),
                  pl.BlockSpec((None, None, d, de), lambda i, te, tv: (layer, te[i], 0, 0)),
                  pl.BlockSpec((None, None, de, d), lambda i, te, tv: (layer, te[i], 0, 0))],
        out_specs=pl.BlockSpec((MOE_TM, d), lambda i, te, tv: (i, 0)),
        scratch_shapes=[pltpu.VMEM((d, de), BF16), pltpu.VMEM((d, de), BF16), pltpu.VMEM((de, d), BF16),
                        pltpu.VMEM((2 * MOE_TM * sub, 128), jnp.uint32), pltpu.SemaphoreType.DMA((2,))],
    )
    return pl.pallas_call(
        _moe_kernel,
        out_shape=_sds((n_rows, d), BF16),
        grid_spec=grid_spec,
        compiler_params=_cparams(("arbitrary",)),
        name="moe_experts",
    )(tile_expert, tile_valid, toks, toks, x_packed.reshape(t, sub, 128), w_gate, w_up, w_down)


def _top_k_desc(vals, k):
    n = vals.shape[-1]
    iota = lax.broadcasted_iota(jnp.int32, vals.shape, vals.ndim - 1)
    top_v, top_i = [], []
    for _ in range(k):
        m = jnp.max(vals, axis=-1, keepdims=True)
        idx = jnp.min(jnp.where(vals == m, iota, n), axis=-1, keepdims=True)
        top_v.append(m)
        top_i.append(idx)
        vals = jnp.where(iota == idx, -jnp.inf, vals)
    return jnp.concatenate(top_v, axis=-1), jnp.concatenate(top_i, axis=-1)


def _route(logits, e_bias):
    t = logits.shape[0]
    scores = jax.nn.sigmoid(logits)
    sel = scores + e_bias.astype(F32)
    per_group = sel.reshape(t, N_GROUPS, N_EXPERTS // N_GROUPS)
    group_score = _top_k_desc(per_group, 2)[0].sum(-1)
    _, g_idx = _top_k_desc(group_score, TOPK_GROUPS)
    g_mask = (g_idx[:, :, None] == jnp.arange(N_GROUPS)[None, None, :]).any(axis=1)
    e_mask = jnp.repeat(g_mask, N_EXPERTS // N_GROUPS, axis=1)
    _, e_idx = _top_k_desc(jnp.where(e_mask, sel, -jnp.inf), TOP_K)
    wts = jnp.take_along_axis(scores, e_idx, axis=1)
    wts = wts / wts.sum(-1, keepdims=True) * ROUTED_SCALE
    return e_idx, wts


def _running_count_kernel(m_ref, o_ref, carry_ref):
    i = pl.program_id(0)

    @pl.when(i == 0)
    def _():
        carry_ref[...] = jnp.zeros_like(carry_ref)

    tm = m_ref.shape[0]
    lower = (lax.broadcasted_iota(jnp.int32, (tm, tm), 0)
             >= lax.broadcasted_iota(jnp.int32, (tm, tm), 1)).astype(BF16)
    counts = _dot(lower, m_ref[...]) + carry_ref[...]
    o_ref[...] = counts
    carry_ref[...] = counts[tm - 1:tm, :]


def _running_count(member, tm):
    t, n = member.shape
    assert t % tm == 0
    return pl.pallas_call(
        _running_count_kernel,
        out_shape=_sds((t, n), F32),
        grid=(t // tm,),
        in_specs=[pl.BlockSpec((tm, n), lambda i: (i, 0))],
        out_specs=pl.BlockSpec((tm, n), lambda i: (i, 0)),
        scratch_shapes=[pltpu.VMEM((1, n), F32)],
        compiler_params=_cparams(("arbitrary",)),
        name="running_count",
    )(member)


def _dispatch_plan(e_idx):
    t = e_idx.shape[0]
    n_assign = t * TOP_K
    member = (e_idx[:, :, None] == jnp.arange(N_EXPERTS)[None, None, :]).any(axis=1)
    running = _running_count(member.astype(BF16), math.gcd(t, 832)).astype(jnp.int32)
    counts = running[-1]
    rank = jnp.take_along_axis(running, e_idx, axis=1) - 1
    padded = (counts + MOE_TM - 1) // MOE_TM * MOE_TM
    pad_end = jnp.cumsum(padded)
    pad_start = pad_end - padded
    dest_of_assign = (pad_start[e_idx] + rank).astype(jnp.int32).reshape(-1)
    n_tiles = (n_assign + N_EXPERTS * (MOE_TM - 1)) // MOE_TM + 1
    tok_of_assign = jnp.repeat(jnp.arange(t, dtype=jnp.int32), TOP_K)
    row_tok = jnp.zeros((n_tiles * MOE_TM,), jnp.int32).at[dest_of_assign].set(tok_of_assign)
    tile_start = jnp.arange(n_tiles) * MOE_TM
    tile_expert = jnp.minimum((tile_start[:, None] >= pad_end[None, :]).sum(axis=1),
                              N_EXPERTS - 1).astype(jnp.int32)
    tile_valid = (tile_start < pad_end[-1]).astype(jnp.int32)
    last_used = tile_expert[jnp.maximum(pad_end[-1] // MOE_TM - 1, 0)]
    tile_expert = jnp.where(tile_valid > 0, tile_expert, last_used)
    return row_tok, tile_expert, tile_valid, dest_of_assign


def _final_kernel(xf_ref, xb_ref, y_ref, wt_ref, p_ref, wsg_ref, wsu_ref, wsd_ref, wpg_ref, wpp_ref,
                  g_ref, b_ref, of_ref, ob_ref):
    wt = wt_ref[...].astype(BF16).astype(F32)
    routed = wt[:, 0:1] * y_ref[0].astype(F32)
    for j in range(1, wt.shape[1]):
        routed = routed + wt[:, j:j + 1] * y_ref[j].astype(F32)
    x = xb_ref[...]
    a = _dot(x, wsg_ref[...])
    u = _dot(x, wsu_ref[...])
    shared = _dot((_silu(a) * u).astype(BF16), wsd_ref[...])
    gate = jax.nn.sigmoid(_dot(x, wpg_ref[...]))
    ple = gate * _dot(p_ref[...], wpp_ref[...])
    v = ALPHA * xf_ref[...] + (routed + shared) + ple
    y = _layer_norm_rows(v, g_ref[...], b_ref[...])
    of_ref[...] = y
    ob_ref[...] = y.astype(BF16)


def _final_block(xf, xb, y_assign, wts, p, wsg, wsu, wsd, wpg, wpp, g, b, tm):
    m, d = xf.shape
    de = wsg.shape[1]
    dp = p.shape[1]
    nk = wts.shape[1]
    assert m % tm == 0
    row = lambda i: (i, 0)
    const = lambda i: (0, 0)
    once = dict(pipeline_mode=pl.Buffered(1))
    return pl.pallas_call(
        _final_kernel,
        out_shape=(_sds((m, d), F32), _sds((m, d), BF16)),
        grid=(m // tm,),
        in_specs=[pl.BlockSpec((tm, d), row), pl.BlockSpec((tm, d), row),
                  pl.BlockSpec((nk, tm, d), lambda i: (0, i, 0)), pl.BlockSpec((tm, nk), row),
                  pl.BlockSpec((tm, dp), row),
                  pl.BlockSpec((d, de), const, **once), pl.BlockSpec((d, de), const, **once),
                  pl.BlockSpec((de, d), const, **once), pl.BlockSpec((d, d), const, **once),
                  pl.BlockSpec((dp, d), const, **once),
                  pl.BlockSpec((1, d), const), pl.BlockSpec((1, d), const)],
        out_specs=(pl.BlockSpec((tm, d), row), pl.BlockSpec((tm, d), row)),
        compiler_params=_cparams(("parallel",)),
        name="shared_ple_ln",
    )(xf, xb, y_assign, wts, p, wsg, wsu, wsd, wpg, wpp, g.reshape(1, d), b.reshape(1, d))


def _attn_prompt_kernel(slopes_ref, q_ref, k_ref, v_ref, o_ref, kk, vv, m_s, l_s, acc_s):
    h = pl.program_id(1)
    sl = pl.program_id(2)

    @pl.when(sl == 0)
    def _():
        kk[0:SLAB, :] = jnp.zeros((SLAB, HEAD_DIM), F32)
        vv[0:SLAB, :] = jnp.zeros((SLAB, HEAD_DIM), F32)

    kk[SLAB:2 * SLAB, :] = k_ref[...]
    vv[SLAB:2 * SLAB, :] = v_ref[...]

    slope = slopes_ref[h]
    ii = lax.broadcasted_iota(jnp.int32, (BAND, BAND), 0)
    jj = lax.broadcasted_iota(jnp.int32, (BAND, BAND), 1)
    dist_prev = (ii + BAND - jj).astype(F32)
    dist_cur = (ii - jj).astype(F32)

    for di, d in enumerate(DILATIONS):
        c = slope * float(d)
        bias_prev = jnp.where(jj >= ii, -c * dist_prev, NEG)
        bias_cur = jnp.where(jj <= ii, -c * dist_cur, NEG)

        def units(it, carry, di=di, d=d, bias_prev=bias_prev, bias_cur=bias_cur):
            def rows(start):
                if d == 1:
                    return pl.ds(start, BAND)
                return pl.ds(start, BAND, stride=d)

            us = [it * ATT_UNROLL + j for j in range(ATT_UNROLL)]
            sp = [u // d for u in us]
            base = [s * (BAND * d) + (u - s * d) for u, s in zip(us, sp)]
            n = range(ATT_UNROLL)
            q = [q_ref[rows(base[j]), :].astype(BF16) for j in n]
            kc = [kk[rows(base[j] + SLAB), :].astype(BF16) for j in n]
            kp = [kk[rows(base[j] + SLAB - BAND * d), :].astype(BF16) for j in n]
            pen = [jnp.where(jnp.logical_and(sl == 0, sp[j] == 0), NEG, 0.0) for j in n]
            s_p = [_dot_nt(q[j], kp[j]) * ATT_SCALE + (bias_prev + pen[j]) for j in n]
            s_c = [_dot_nt(q[j], kc[j]) * ATT_SCALE + bias_cur for j in n]
            m = [jnp.maximum(jnp.max(s_p[j], axis=-1, keepdims=True), jnp.max(s_c[j], axis=-1, keepdims=True))
                 for j in n]
            p_p = [jnp.exp(s_p[j] - m[j]) for j in n]
            p_c = [jnp.exp(s_c[j] - m[j]) for j in n]
            l = [jnp.sum(p_p[j], axis=-1, keepdims=True) + jnp.sum(p_c[j], axis=-1, keepdims=True) for j in n]
            vc = [vv[rows(base[j] + SLAB), :].astype(BF16) for j in n]
            vp = [vv[rows(base[j] + SLAB - BAND * d), :].astype(BF16) for j in n]
            acc = [_dot(p_p[j].astype(BF16), vp[j]) + _dot(p_c[j].astype(BF16), vc[j]) for j in n]
            for j in n:
                m_s[di, rows(base[j]), :] = jnp.broadcast_to(m[j], (BAND, HEAD_DIM))
                l_s[di, rows(base[j]), :] = jnp.broadcast_to(l[j], (BAND, HEAD_DIM))
                acc_s[di, rows(base[j]), :] = acc[j]
            return carry

        lax.fori_loop(0, SLAB // BAND // ATT_UNROLL, units, 0)

    def merge(t, carry):
        rows = pl.ds(pl.multiple_of(t * BAND, BAND), BAND)
        m_max = jnp.maximum(jnp.maximum(m_s[0, rows, :], m_s[1, rows, :]), m_s[2, rows, :])
        w0 = jnp.exp(m_s[0, rows, :] - m_max)
        num = w0 * acc_s[0, rows, :]
        den = w0 * l_s[0, rows, :]
        for di in range(1, len(DILATIONS)):
            w = jnp.exp(m_s[di, rows, :] - m_max)
            num = num + w * acc_s[di, rows, :]
            den = den + w * l_s[di, rows, :]
        o_ref[rows, :] = (num / den).astype(o_ref.dtype)
        return carry

    lax.fori_loop(0, SLAB // BAND, merge, 0)
    kk[0:SLAB, :] = kk[SLAB:2 * SLAB, :]
    vv[0:SLAB, :] = vv[SLAB:2 * SLAB, :]


def _attn_prompt(proj, slopes, n_batch, seq):
    assert seq % SLAB == 0
    ns = seq // SLAB
    grid_spec = pltpu.PrefetchScalarGridSpec(
        num_scalar_prefetch=1,
        grid=(n_batch, ATT_HEADS, ns),
        in_specs=[pl.BlockSpec((SLAB, HEAD_DIM), lambda b, h, s, sl: (b * ns + s, h)),
                  pl.BlockSpec((SLAB, HEAD_DIM), lambda b, h, s, sl: (b * ns + s, ATT_HEADS + h)),
                  pl.BlockSpec((SLAB, HEAD_DIM), lambda b, h, s, sl: (b * ns + s, 2 * ATT_HEADS + h))],
        out_specs=pl.BlockSpec((SLAB, HEAD_DIM), lambda b, h, s, sl: (b * ns + s, h)),
        scratch_shapes=[pltpu.VMEM((2 * SLAB, HEAD_DIM), F32), pltpu.VMEM((2 * SLAB, HEAD_DIM), F32),
                        pltpu.VMEM((len(DILATIONS), SLAB, HEAD_DIM), F32),
                        pltpu.VMEM((len(DILATIONS), SLAB, HEAD_DIM), F32),
                        pltpu.VMEM((len(DILATIONS), SLAB, HEAD_DIM), F32)],
    )
    return pl.pallas_call(
        _attn_prompt_kernel,
        out_shape=_sds((n_batch * seq, ATT_W), BF16),
        grid_spec=grid_spec,
        compiler_params=_cparams(("parallel", "parallel", "arbitrary")),
        name="attn_prompt",
    )(slopes, proj, proj, proj)


def _attn_sample_kernel(slopes_ref, q_ref, kn_ref, vn_ref, k_ref, v_ref, o_ref):
    hb = pl.program_id(1)
    n_heads, n_buf = k_ref.shape[0], k_ref.shape[1]
    steps = (BAND - lax.broadcasted_iota(jnp.int32, (BAND, 1), 0)).astype(F32)
    for hh in range(n_heads):
        h = hb * n_heads + hh
        head = pl.ds(h, 1)
        q = q_ref[head, :]
        v_self = vn_ref[head, :]
        slope = slopes_ref[h]
        s_self = jnp.sum(q * kn_ref[head, :], axis=-1, keepdims=True) * ATT_SCALE
        parts = []
        for d in DILATIONS:
            start = n_buf - BAND * d
            rows = pl.ds(start, BAND) if d == 1 else pl.ds(start, BAND, stride=d)
            sc = (jnp.sum(k_ref[hh, rows, :] * q, axis=-1, keepdims=True) * ATT_SCALE
                  - slope * (steps * float(d)))
            m_d = jnp.maximum(jnp.max(sc, axis=0, keepdims=True), s_self)
            p = jnp.exp(sc - m_d)
            p_self = jnp.exp(s_self - m_d)
            l_d = jnp.sum(p, axis=0, keepdims=True) + p_self
            acc = jnp.sum(p * v_ref[hh, rows, :], axis=0, keepdims=True) + p_self * v_self
            parts.append((m_d, l_d, acc))
        m_max = jnp.maximum(jnp.maximum(parts[0][0], parts[1][0]), parts[2][0])
        num = jnp.zeros(q.shape, F32)
        den = jnp.zeros(s_self.shape, F32)
        for m_d, l_d, acc in parts:
            w = jnp.exp(m_d - m_max)
            num = num + w * acc
            den = den + w * l_d
        o_ref[head, :] = num / den


def _attn_sample(q, k_new, v_new, k_buf, v_buf, slopes):
    nb, nh, n_buf, e = k_buf.shape
    hpb = math.gcd(nh, ATT_SAMPLE_HEADS)
    assert n_buf >= BAND * DILATIONS[-1]
    tok = pl.BlockSpec((None, nh, e), lambda b, h, sl: (b, 0, 0))
    buf = pl.BlockSpec((None, hpb, n_buf, e), lambda b, h, sl: (b, h, 0, 0))
    grid_spec = pltpu.PrefetchScalarGridSpec(
        num_scalar_prefetch=1, grid=(nb, nh // hpb), in_specs=[tok, tok, tok, buf, buf], out_specs=tok)
    return pl.pallas_call(
        _attn_sample_kernel,
        out_shape=_sds((nb, nh, e), F32),
        grid_spec=grid_spec,
        compiler_params=_cparams(("parallel", "arbitrary")),
        name="attn_sample",
    )(slopes, q, k_new, v_new, k_buf, v_buf)


def _shifted(u, prev8, shift):
    row = lax.broadcasted_iota(jnp.int32, u.shape, 0)
    out = pltpu.roll(u, shift, 0)
    for s in range(shift):
        out = jnp.where(row == s, prev8[8 - shift + s:8 - shift + s + 1, :], out)
    return out


def _sconv_prompt_kernel(h_ref, gb_ref, gc_ref, hp_ref, gcp_ref, w_ref, o_ref):
    i = pl.program_id(1)
    u = gc_ref[...] * h_ref[...]
    up = jnp.where(i == 0, 0.0, gcp_ref[...] * hp_ref[...])
    w = w_ref[...]
    conv = w[0:1] * _shifted(u, up, 2) + w[1:2] * _shifted(u, up, 1) + w[2:3] * u
    o_ref[...] = (gb_ref[...] * conv).astype(o_ref.dtype)


def _sconv_prompt(proj, w, n_batch, seq, tm):
    sc = w.shape[1]
    nt = seq // tm
    cb = 3 * ATT_W // sc
    cur = lambda c: pl.BlockSpec((tm, sc), lambda b, i: (b * nt + i, cb + c))
    prev = lambda c: pl.BlockSpec((8, sc), lambda b, i: (jnp.maximum((b * nt + i) * (tm // 8) - 1, 0), cb + c))
    return pl.pallas_call(
        _sconv_prompt_kernel,
        out_shape=_sds((n_batch * seq, sc), BF16),
        grid=(n_batch, nt),
        in_specs=[cur(0), cur(1), cur(2), prev(0), prev(2), pl.BlockSpec((SC_WIDTH, sc), lambda b, i: (0, 0))],
        out_specs=pl.BlockSpec((tm, sc), lambda b, i: (b * nt + i, 0)),
        compiler_params=_cparams(("parallel", "parallel")),
        name="sconv_prompt",
    )(proj, proj, proj, proj, proj, w)


def _sconv_sample_kernel(h_ref, gb_ref, gc_ref, p0_ref, p1_ref, w_ref, o_ref, u_ref):
    u = gc_ref[...] * h_ref[...]
    w = w_ref[...]
    conv = w[0:1] * p0_ref[...] + w[1:2] * p1_ref[...] + w[2:3] * u
    o_ref[...] = (gb_ref[...] * conv).astype(o_ref.dtype)
    u_ref[...] = u


def _sconv_sample(h, gb, gc, p0, p1, w):
    nb, sc = h.shape
    return pl.pallas_call(
        _sconv_sample_kernel,
        out_shape=(_sds((nb, sc), BF16), _sds((nb, sc), F32)),
        name="sconv_sample",
    )(h, gb, gc, p0, p1, w)


def _qk_normalise(y, cb, n_qk_blocks):
    scale = jnp.where(cb < n_qk_blocks // 2, DN_DIM ** -0.5, 1.0)
    segs = []
    for s in range(y.shape[1] // DN_DIM):
        seg = y[:, s * DN_DIM:(s + 1) * DN_DIM]
        ss = jnp.sum(seg * seg, axis=-1, keepdims=True)
        segs.append(seg * (lax.rsqrt(ss + L2_EPS) * scale))
    return jnp.concatenate(segs, axis=1)


def _dconv_prompt_kernel(x_ref, xp_ref, w_ref, o_ref, *, n_qk_blocks):
    cb = pl.program_id(0)
    i = pl.program_id(2)
    x = x_ref[...]
    xp = jnp.where(i == 0, 0.0, xp_ref[...])
    w = w_ref[...]
    conv = (w[0:1] * _shifted(x, xp, 3) + w[1:2] * _shifted(x, xp, 2)
            + w[2:3] * _shifted(x, xp, 1) + w[3:4] * x)
    y = _silu(conv)

    @pl.when(cb < n_qk_blocks)
    def _():
        o_ref[...] = _qk_normalise(y, cb, n_qk_blocks)

    @pl.when(cb >= n_qk_blocks)
    def _():
        o_ref[...] = y


def _dconv_prompt(proj, w, n_batch, seq, tm, tc):
    nt = seq // tm
    ncb = DN_CONV_DIM // tc
    return pl.pallas_call(
        functools.partial(_dconv_prompt_kernel, n_qk_blocks=2 * DN_QK_W // tc),
        out_shape=_sds((n_batch * seq, DN_CONV_DIM), F32),
        grid=(ncb, n_batch, nt),
        in_specs=[pl.BlockSpec((tm, tc), lambda c, b, i: (b * nt + i, c)),
                  pl.BlockSpec((8, tc), lambda c, b, i: (jnp.maximum((b * nt + i) * (tm // 8) - 1, 0), c)),
                  pl.BlockSpec((DN_CONV_WIDTH, tc), lambda c, b, i: (0, c))],
        out_specs=pl.BlockSpec((tm, tc), lambda c, b, i: (b * nt + i, c)),
        compiler_params=_cparams(("parallel", "parallel", "parallel")),
        name="dconv_prompt",
    )(proj, proj, w)


def _dconv_sample_kernel(x_ref, p0_ref, p1_ref, p2_ref, w_ref, o_ref, *, n_qk_blocks):
    cb = pl.program_id(0)
    w = w_ref[...]
    conv = w[0:1] * p0_ref[...] + w[1:2] * p1_ref[...] + w[2:3] * p2_ref[...] + w[3:4] * x_ref[...]
    y = _silu(conv)

    @pl.when(cb < n_qk_blocks)
    def _():
        o_ref[...] = _qk_normalise(y, cb, n_qk_blocks)

    @pl.when(cb >= n_qk_blocks)
    def _():
        o_ref[...] = y


def _dconv_sample(x, p0, p1, p2, w, tc):
    nb = x.shape[0]
    blk = pl.BlockSpec((nb, tc), lambda c: (0, c))
    return pl.pallas_call(
        functools.partial(_dconv_sample_kernel, n_qk_blocks=2 * DN_QK_W // tc),
        out_shape=_sds((nb, DN_CONV_DIM), F32),
        grid=(DN_CONV_DIM // tc,),
        in_specs=[blk, blk, blk, blk, pl.BlockSpec((DN_CONV_WIDTH, tc), lambda c: (0, c))],
        out_specs=blk,
        compiler_params=_cparams(("parallel",)),
        name="dconv_sample",
    )(x, p0, p1, p2, w)


def _gated_rms(o, z, norm_w):
    return o * lax.rsqrt(jnp.mean(o * o, axis=-1, keepdims=True) + RMS_EPS) * norm_w * _silu(z)


def _delta_prompt_kernel(q_ref, k_ref, v_ref, z_ref, g_ref, gt_ref, beta_ref, nw_ref, o_ref, s_out_ref, s_ref,
                         *, n_chunks):
    c = pl.program_id(2)

    @pl.when(c == 0)
    def _():
        s_ref[...] = jnp.zeros_like(s_ref)

    cc = DN_CHUNK
    heads = range(DN_GROUP_V)
    ii = lax.broadcasted_iota(jnp.int32, (cc, cc), 0)
    jj = lax.broadcasted_iota(jnp.int32, (cc, cc), 1)
    beta_all = beta_ref[...]
    norm_w = nw_ref[...]

    lower = (ii >= jj).astype(BF16)
    gc_all = _dot_exact_lhs(lower, g_ref[...])
    kk = lax.broadcasted_iota(jnp.int32, (2 * cc, cc), 0) - lax.rem(c, 2) * cc
    tt = lax.broadcasted_iota(jnp.int32, (2 * cc, cc), 1)
    upto = jnp.logical_and(kk >= 0, kk <= tt).astype(BF16)
    gr_all = _dot_exact_rhs(gt_ref[...], upto)

    q = [q_ref[:, h * DN_DIM:(h + 1) * DN_DIM] for h in range(DN_GROUP_QK)]
    k = [k_ref[:, h * DN_DIM:(h + 1) * DN_DIM] for h in range(DN_GROUP_QK)]
    k_b = [x.astype(BF16) for x in k]
    qk = [_dot_nt(q[h].astype(BF16), k_b[h]) for h in range(DN_GROUP_QK)]

    gc = [gc_all[:, h:h + 1] for h in heads]
    beta = [beta_all[:, h:h + 1] for h in heads]
    decay = [jnp.where(ii >= jj, jnp.exp(jnp.minimum(gc[h] - gr_all[h:h + 1, :], 0.0)), 0.0) for h in heads]
    eg = [jnp.exp(gc[h]) for h in heads]
    kbeta = [k[h // 2] * beta[h] for h in heads]
    npow = [-jnp.where(ii > jj, _dot_nt(kbeta[h].astype(BF16), k_b[h // 2]) * decay[h], 0.0) for h in heads]
    y = [jnp.concatenate([v_ref[:, h * DN_DIM:(h + 1) * DN_DIM] * beta[h], kbeta[h] * eg[h]], axis=1)
         for h in heads]
    for step in range(6):
        upd = [_dot3(npow[h], y[h]) for h in heads]
        y = [y[h] + upd[h] for h in heads]
        if step < 5:
            npow = [_dot3(npow[h], npow[h]) for h in heads]
    s_old = [s_ref[h] for h in heads]
    s_b = [x.astype(BF16) for x in s_old]
    v_new = [y[h][:, :DN_DIM] - _dot(y[h][:, DN_DIM:].astype(BF16), s_b[h]) for h in heads]
    v_new_b = [x.astype(BF16) for x in v_new]
    o = [_dot((q[h // 2] * eg[h]).astype(BF16), s_b[h]) + _dot((qk[h // 2] * decay[h]).astype(BF16), v_new_b[h])
         for h in heads]
    for h in heads:
        g_last = gc[h][cc - 1:cc, :]
        k_dec = k[h // 2] * jnp.exp(g_last - gc[h])
        s_ref[h] = s_old[h] * jnp.exp(g_last) + _dot_tn(k_dec.astype(BF16), v_new_b[h])
    for h in heads:
        z = z_ref[:, h * DN_DIM:(h + 1) * DN_DIM]
        o_ref[:, h * DN_DIM:(h + 1) * DN_DIM] = _gated_rms(o[h], z, norm_w).astype(o_ref.dtype)

    @pl.when(c == n_chunks - 1)
    def _():
        s_out_ref[...] = s_ref[...]


def _delta_prompt(qkv, proj, g, g_t, beta, norm_w, n_batch, seq):
    assert seq % (2 * DN_CHUNK) == 0
    nc = seq // DN_CHUNK
    n_hg = DN_QK_HEADS // DN_GROUP_QK
    wq = DN_GROUP_QK * DN_DIM
    wv = DN_GROUP_V * DN_DIM
    row = lambda b, hg, c: b * nc + c
    return pl.pallas_call(
        functools.partial(_delta_prompt_kernel, n_chunks=nc),
        out_shape=(_sds((n_batch * seq, DN_VW), BF16), _sds((n_batch, DN_V_HEADS, DN_DIM, DN_DIM), F32)),
        grid=(n_batch, n_hg, nc),
        in_specs=[pl.BlockSpec((DN_CHUNK, wq), lambda b, hg, c: (row(b, hg, c), hg)),
                  pl.BlockSpec((DN_CHUNK, wq), lambda b, hg, c: (row(b, hg, c), DN_QK_W // wq + hg)),
                  pl.BlockSpec((DN_CHUNK, wv), lambda b, hg, c: (row(b, hg, c), 2 * DN_QK_W // wv + hg)),
                  pl.BlockSpec((DN_CHUNK, wv), lambda b, hg, c: (row(b, hg, c), DN_CONV_DIM // wv + hg)),
                  pl.BlockSpec((None, DN_CHUNK, DN_GROUP_V), lambda b, hg, c: (hg, row(b, hg, c), 0)),
                  pl.BlockSpec((None, DN_GROUP_V, 2 * DN_CHUNK), lambda b, hg, c: (hg, 0, row(b, hg, c) // 2)),
                  pl.BlockSpec((None, DN_CHUNK, DN_GROUP_V), lambda b, hg, c: (hg, row(b, hg, c), 0)),
                  pl.BlockSpec((1, DN_DIM), lambda b, hg, c: (0, 0))],
        out_specs=(pl.BlockSpec((DN_CHUNK, wv), lambda b, hg, c: (row(b, hg, c), hg)),
                   pl.BlockSpec((None, DN_GROUP_V, DN_DIM, DN_DIM), lambda b, hg, c: (b, hg, 0, 0))),
        scratch_shapes=[pltpu.VMEM((DN_GROUP_V, DN_DIM, DN_DIM), F32)],
        compiler_params=_cparams(("parallel", "parallel", "arbitrary")),
        name="delta_prompt",
    )(qkv, qkv, qkv, proj, g, g_t, beta, norm_w.reshape(1, DN_DIM))


def _delta_sample_kernel(qt_ref, kt_ref, v_ref, z_ref, g_ref, beta_ref, nw_ref, s_ref, o_ref, s_out_ref):
    norm_w = nw_ref[...]
    qt = qt_ref[...]
    kt = kt_ref[...]
    for hv in range(DN_V_HEADS):
        hq = hv // 2
        k_col = jnp.broadcast_to(kt[:, hq:hq + 1], (DN_DIM, DN_DIM))
        q_col = jnp.broadcast_to(qt[:, hq:hq + 1], (DN_DIM, DN_DIM))
        eg = jnp.exp(g_ref[hv:hv + 1, :])
        beta = beta_ref[hv:hv + 1, :]
        s = s_ref[hv]
        sk = jnp.sum(s * k_col, axis=0, keepdims=True)
        delta = beta * (v_ref[hv:hv + 1, :] - eg * sk)
        s_new = s * eg + k_col * delta
        s_out_ref[hv] = s_new
        o = jnp.sum(s_new * q_col, axis=0, keepdims=True)
        o_ref[hv:hv + 1, :] = _gated_rms(o, z_ref[hv:hv + 1, :], norm_w).astype(o_ref.dtype)


def _delta_sample(qt, kt, v, z, g_b, beta_b, norm_w, s0):
    nb = v.shape[0]
    per_b3 = lambda shape: pl.BlockSpec((None,) + shape, lambda b: (b, 0, 0))
    st = pl.BlockSpec((None, DN_V_HEADS, DN_DIM, DN_DIM), lambda b: (b, 0, 0, 0))
    return pl.pallas_call(
        _delta_sample_kernel,
        out_shape=(_sds((nb, DN_V_HEADS, DN_DIM), BF16), _sds(s0.shape, F32)),
        grid=(nb,),
        in_specs=[per_b3((DN_DIM, DN_QK_HEADS)), per_b3((DN_DIM, DN_QK_HEADS)),
                  per_b3((DN_V_HEADS, DN_DIM)), per_b3((DN_V_HEADS, DN_DIM)),
                  per_b3((DN_V_HEADS, DN_DIM)), per_b3((DN_V_HEADS, DN_DIM)),
                  pl.BlockSpec((1, DN_DIM), lambda b: (0, 0)), st],
        out_specs=(per_b3((DN_V_HEADS, DN_DIM)), st),
        compiler_params=_cparams(("parallel",)),
        name="delta_sample",
    )(qt, kt, v, z, g_b, beta_b, norm_w.reshape(1, DN_DIM), s0)


def _mixer_layer(xf, xb, w_in, sconv_w, w_out, k_buf, v_buf, sconv_prev, n_batch, seq, g1, b1):
    tp = n_batch * seq
    proj = _matmul(xb, w_in.astype(BF16), 832, 512)
    slopes = jnp.exp2(-8.0 * (jnp.arange(ATT_HEADS, dtype=F32) + 1.0) / ATT_HEADS)
    att_p = _attn_prompt(proj, slopes, n_batch, seq)
    ns = xf.shape[0] - tp
    ps = proj[tp:]
    q_s, k_s, v_s = (ps[:, i * ATT_W:(i + 1) * ATT_W] for i in range(3))
    heads = lambda a: a.reshape(ns, ATT_HEADS, HEAD_DIM)
    att_s = _attn_sample(heads(q_s), heads(k_s), heads(v_s), k_buf.transpose(0, 2, 1, 3),
                         v_buf.transpose(0, 2, 1, 3), slopes).astype(BF16)
    sc = sconv_w.shape[1]
    gc_p = _sconv_prompt(proj, sconv_w, n_batch, seq, 512)
    h_s, gb_s, gcs_s = (ps[:, 3 * ATT_W + i * sc:3 * ATT_W + (i + 1) * sc] for i in range(3))
    gc_s, u_s = _sconv_sample(h_s, gb_s, gcs_s, sconv_prev[:, 0], sconv_prev[:, 1], sconv_w)
    mixed = jnp.concatenate([jnp.concatenate([att_p, att_s.reshape(ns, ATT_W)], axis=0),
                             jnp.concatenate([gc_p, gc_s], axis=0)], axis=1)
    x1 = _matmul_residual_ln(mixed, w_out.astype(BF16), xf, g1, b1, 320, w_out.shape[0])

    keep = min(SLAB, seq)
    tail = lambda n, c0, c1: jnp.stack([proj[(b + 1) * seq - n:(b + 1) * seq, c0:c1] for b in range(n_batch)])
    k_rows_p = tail(keep, ATT_W, 2 * ATT_W).reshape(n_batch, keep, ATT_HEADS, HEAD_DIM)
    v_rows_p = tail(keep, 2 * ATT_W, 3 * ATT_W).reshape(n_batch, keep, ATT_HEADS, HEAD_DIM)
    u_tail = (tail(SC_WIDTH - 1, 3 * ATT_W + 2 * sc, 3 * ATT_W + 3 * sc)
              * tail(SC_WIDTH - 1, 3 * ATT_W, 3 * ATT_W + sc))
    sconv_new_s = jnp.stack([sconv_prev[:, 1], u_s], axis=1)
    outs_p = (k_rows_p, v_rows_p, u_tail)
    outs_s = (k_s.reshape(ns, 1, ATT_HEADS, HEAD_DIM), v_s.reshape(ns, 1, ATT_HEADS, HEAD_DIM), sconv_new_s)
    return x1, outs_p, outs_s


def _delta_layer(xf, xb, w_in, conv_w, a_log, dt_bias, norm_w, w_out, conv_prev, s0, n_batch, seq, g1, b1):
    tp = n_batch * seq
    ns = xf.shape[0] - tp
    n_main = DN_CONV_DIM + DN_VW
    proj = _matmul(xb, w_in[:, :n_main].astype(BF16), 832, 512)
    ba = _matmul(xb, w_in[:, n_main:].astype(BF16), 832, 2 * DN_V_HEADS)
    beta = jax.nn.sigmoid(ba[:, :DN_V_HEADS])
    g = -jnp.exp(a_log.astype(F32)) * jax.nn.softplus(ba[:, DN_V_HEADS:] + dt_bias.astype(F32))

    qkv_p = _dconv_prompt(proj, conv_w, n_batch, seq, 512, 512)
    n_vg = DN_V_HEADS // DN_GROUP_V
    grp = lambda a: a[:tp].reshape(tp, n_vg, DN_GROUP_V).transpose(1, 0, 2)
    o_p, s_p = _delta_prompt(qkv_p, proj, grp(g), grp(g).transpose(0, 2, 1), grp(beta), norm_w, n_batch, seq)

    ps = proj[tp:]
    qkv_s = _dconv_sample(ps[:, :DN_CONV_DIM], conv_prev[:, 0], conv_prev[:, 1], conv_prev[:, 2], conv_w, 512)
    qt = qkv_s[:, :DN_QK_W].reshape(ns, DN_QK_HEADS, DN_DIM).transpose(0, 2, 1)
    kt = qkv_s[:, DN_QK_W:2 * DN_QK_W].reshape(ns, DN_QK_HEADS, DN_DIM).transpose(0, 2, 1)
    v_s = qkv_s[:, 2 * DN_QK_W:].reshape(ns, DN_V_HEADS, DN_DIM)
    z_s = ps[:, DN_CONV_DIM:].reshape(ns, DN_V_HEADS, DN_DIM)
    lanes = lambda a: jnp.broadcast_to(a[tp:, :, None], (ns, DN_V_HEADS, DN_DIM))
    o_s, s_s = _delta_sample(qt, kt, v_s, z_s, lanes(g), lanes(beta), norm_w, s0)

    og = jnp.concatenate([o_p, o_s.reshape(ns, DN_VW)], axis=0)
    x1 = _matmul_residual_ln(og, w_out.astype(BF16), xf, g1, b1, 320, 2048)

    dconv_new_p = jnp.stack([proj[(b + 1) * seq - (DN_CONV_WIDTH - 1):(b + 1) * seq, :DN_CONV_DIM]
                             for b in range(n_batch)])
    dconv_new_s = jnp.concatenate([conv_prev[:, 1:], ps[:, None, :DN_CONV_DIM]], axis=1)
    return x1, (dconv_new_p, s_p), (dconv_new_s, s_s)


def _moe_block(x1, p_b, w_router, e_bias, w_gate, w_up, w_down, layer, ws_gate, ws_up, ws_down,
               ple_w_proj, ple_w_gate, g2, b2):
    x1f, x1b, x1p = x1
    t, d = x1f.shape
    logits = _matmul(x1b, w_router.astype(BF16), 832, N_EXPERTS)
    e_idx, wts = _route(logits, e_bias)
    row_tok, tile_expert, tile_valid, dest_of_assign = _dispatch_plan(e_idx)
    y_rows = _moe_experts(x1p, row_tok, tile_expert, tile_valid, w_gate, w_up, w_down, layer)
    y_assign = y_rows[dest_of_assign.reshape(t, TOP_K).T.reshape(-1)].reshape(TOP_K, t, d)
    return _final_block(x1f, x1b, y_assign, wts, p_b, ws_gate.astype(BF16), ws_up.astype(BF16),
                        ws_down.astype(BF16), ple_w_gate.astype(BF16), ple_w_proj.astype(BF16), g2, b2, 160)


def kernel(x_prompt, x_sample, cache_attn_k, cache_attn_v, state_sconv, state_dconv, state_delta,
           p_prompt, p_sample, mix_w_in, mix_sconv_w, mix_w_out, dn_w_in, dn_conv_w, dn_a_log,
           dn_dt_bias, dn_norm_w, dn_w_out, ln1_g, ln1_b, ln2_g, ln2_b, moe_w_router, moe_e_bias,
           moe_w_gate, moe_w_up, moe_w_down, moe_ws_gate, moe_ws_up, moe_ws_down, ple_w_proj,
           ple_w_gate):
    n_batch, seq, d = x_prompt.shape
    ns = x_sample.shape[0]
    assert x_sample.shape[1] == 1
    tp = n_batch * seq
    xf = jnp.concatenate([x_prompt.reshape(tp, d), x_sample.reshape(ns, d)], axis=0)
    xb = xf.astype(BF16)
    p_all = jnp.concatenate([p_prompt.reshape(DEPTH, tp, -1), p_sample.reshape(DEPTH, ns, -1)],
                            axis=1).astype(BF16)

    mix_p, mix_s, dn_p, dn_s = [], [], [], []
    for i in range(DEPTH):
        j = i // 2
        if i % 2 == 0:
            x1, o_p, o_s = _mixer_layer(
                xf, xb, mix_w_in[j], mix_sconv_w[j], mix_w_out[j], cache_attn_k[j], cache_attn_v[j],
                state_sconv[j], n_batch, seq, ln1_g[i], ln1_b[i])
            mix_p.append(o_p)
            mix_s.append(o_s)
        else:
            x1, o_p, o_s = _delta_layer(
                xf, xb, dn_w_in[j], dn_conv_w[j], dn_a_log[j], dn_dt_bias[j], dn_norm_w[j], dn_w_out[j],
                state_dconv[j], state_delta[j], n_batch, seq, ln1_g[i], ln1_b[i])
            dn_p.append(o_p)
            dn_s.append(o_s)
        xf, xb = _moe_block(x1, p_all[i], moe_w_router[i], moe_e_bias[i], moe_w_gate,
                            moe_w_up, moe_w_down, i, moe_ws_gate[i], moe_ws_up[i], moe_ws_down[i],
                            ple_w_proj[i], ple_w_gate[i], ln2_g[i], ln2_b[i])

    stack = lambda parts, idx: jnp.stack([p[idx] for p in parts])
    y_prompt = xf[:tp].reshape(n_batch, seq, d)
    y_sample = xf[tp:].reshape(ns, 1, d)
    return (y_prompt, y_sample,
            stack(mix_p, 0), stack(mix_p, 1), stack(mix_p, 2), stack(dn_p, 0), stack(dn_p, 1),
            stack(mix_s, 0), stack(mix_s, 1), stack(mix_s, 2), stack(dn_s, 0), stack(dn_s, 1))
```

```python
import functools
import math

import jax
import jax.numpy as jnp
from jax import lax
from jax.experimental import pallas as pl
from jax.experimental.pallas import tpu as pltpu

F32 = jnp.float32
BF16 = jnp.bfloat16

DEPTH = 2
HEAD_DIM = 128
ATT_HEADS = 12
ATT_W = ATT_HEADS * HEAD_DIM
SC_WIDTH = 3
DILATIONS = (1, 4, 16)
BAND = 128
SLAB = BAND * DILATIONS[-1]
ATT_SCALE = HEAD_DIM ** -0.5
ATT_UNROLL = 4
ATT_SAMPLE_HEADS = 4
DN_QK_HEADS = 16
DN_V_HEADS = 32
DN_DIM = 128
DN_QK_W = DN_QK_HEADS * DN_DIM
DN_VW = DN_V_HEADS * DN_DIM
DN_CONV_DIM = 2 * DN_QK_W + DN_VW
DN_CONV_WIDTH = 4
DN_CHUNK = 64
DN_GROUP_QK = 4
DN_GROUP_V = 2 * DN_GROUP_QK
N_EXPERTS = 64
N_GROUPS = 8
TOPK_GROUPS = 4
TOP_K = 8
ROUTED_SCALE = 2.5
ALPHA = (2 * DEPTH) ** 0.25
LN_EPS = 1e-5
RMS_EPS = 1e-6
L2_EPS = 1e-6
NEG = -1e30

MOE_TM = 256
VMEM_LIMIT = 56 * 1024 * 1024


def _sds(shape, dtype):
    return jax.ShapeDtypeStruct(shape, dtype)


def _cparams(sem):
    return pltpu.CompilerParams(dimension_semantics=sem, vmem_limit_bytes=VMEM_LIMIT)


def _silu(x):
    return x * jax.nn.sigmoid(x)


def _layer_norm_rows(v, g, b):
    mu = jnp.mean(v, axis=-1, keepdims=True)
    c = v - mu
    var = jnp.mean(c * c, axis=-1, keepdims=True)
    return c * lax.rsqrt(var + LN_EPS) * g + b


def _split3(x):
    hi = x.astype(BF16)
    r1 = x - hi.astype(F32)
    mid = r1.astype(BF16)
    lo = (r1 - mid.astype(F32)).astype(BF16)
    return hi, mid, lo


def _dot(a, b):
    return jnp.dot(a, b, preferred_element_type=F32)


def _dot_nt(a, b):
    return lax.dot_general(a, b, (((1,), (1,)), ((), ())), preferred_element_type=F32)


def _dot_tn(a, b):
    return lax.dot_general(a, b, (((0,), (0,)), ((), ())), preferred_element_type=F32)


def _dot_exact_lhs(sel_bf16, x):
    hi, mid, lo = _split3(x)
    return _dot(sel_bf16, hi) + _dot(sel_bf16, mid) + _dot(sel_bf16, lo)


def _dot_exact_rhs(x, sel_bf16):
    hi, mid, lo = _split3(x)
    return _dot(hi, sel_bf16) + _dot(mid, sel_bf16) + _dot(lo, sel_bf16)


def _dot3(a, b):
    a_hi = a.astype(BF16)
    a_lo = (a - a_hi.astype(F32)).astype(BF16)
    b_hi = b.astype(BF16)
    b_lo = (b - b_hi.astype(F32)).astype(BF16)
    return _dot(a_hi, b_hi) + _dot(a_hi, b_lo) + _dot(a_lo, b_hi)


def _mm_kernel(x_ref, w_ref, o_ref):
    o_ref[...] = _dot(x_ref[...], w_ref[...])


def _matmul(x, w, tm, tn):
    m, k = x.shape
    n = w.shape[1]
    assert m % tm == 0 and n % tn == 0
    return pl.pallas_call(
        _mm_kernel,
        out_shape=_sds((m, n), F32),
        grid=(m // tm, n // tn),
        in_specs=[pl.BlockSpec((tm, k), lambda i, j: (i, 0)),
                  pl.BlockSpec((k, tn), lambda i, j: (0, j))],
        out_specs=pl.BlockSpec((tm, tn), lambda i, j: (i, j)),
        compiler_params=_cparams(("parallel", "parallel")),
        name="matmul",
    )(x, w)


PACK_SUB = 8


def _pack_rows(y):
    half = y.shape[1] // 2
    yb = y.astype(BF16).astype(F32)
    lo = lax.shift_right_logical(pltpu.bitcast(yb[:, :half], jnp.uint32), jnp.uint32(16))
    hi = pltpu.bitcast(yb[:, half:], jnp.uint32) & jnp.uint32(0xFFFF0000)
    return lo | hi


def _unpack_rows(parts):
    lo = [pltpu.bitcast(lax.shift_left(w, jnp.uint32(16)), F32) for w in parts]
    hi = [pltpu.bitcast(w & jnp.uint32(0xFFFF0000), F32) for w in parts]
    return jnp.concatenate(lo + hi, axis=1).astype(BF16)


def _mm_ln_kernel(a_ref, w_ref, x_ref, g_ref, b_ref, of_ref, ob_ref, op_ref, acc_ref, *, nk):
    kk = pl.program_id(1)
    part = _dot(a_ref[...], w_ref[...])

    @pl.when(kk == 0)
    def _():
        acc_ref[...] = part

    @pl.when(kk > 0)
    def _():
        acc_ref[...] += part

    @pl.when(kk == nk - 1)
    def _():
        y = _layer_norm_rows(ALPHA * x_ref[...] + acc_ref[...], g_ref[...], b_ref[...])
        of_ref[...] = y
        ob_ref[...] = y.astype(BF16)
        packed = _pack_rows(y)
        tm = y.shape[0]
        for s in range(PACK_SUB):
            op_ref[pl.ds(s, tm, stride=PACK_SUB), :] = packed[:, s * 128:(s + 1) * 128]


def _matmul_residual_ln(a, w, x, g, b, tm, tk):
    m, k = a.shape
    d = w.shape[1]
    assert m % tm == 0 and k % tk == 0 and d == 2 * PACK_SUB * 128
    nk = k // tk
    return pl.pallas_call(
        functools.partial(_mm_ln_kernel, nk=nk),
        out_shape=(_sds((m, d), F32), _sds((m, d), BF16), _sds((m * PACK_SUB, 128), jnp.uint32)),
        grid=(m // tm, nk),
        in_specs=[pl.BlockSpec((tm, tk), lambda i, kk: (i, kk)),
                  pl.BlockSpec((tk, d), lambda i, kk: (kk, 0)),
                  pl.BlockSpec((tm, d), lambda i, kk: (i, 0)),
                  pl.BlockSpec((1, d), lambda i, kk: (0, 0)),
                  pl.BlockSpec((1, d), lambda i, kk: (0, 0))],
        out_specs=(pl.BlockSpec((tm, d), lambda i, kk: (i, 0)),
                   pl.BlockSpec((tm, d), lambda i, kk: (i, 0)),
                   pl.BlockSpec((tm * PACK_SUB, 128), lambda i, kk: (i, 0))),
        scratch_shapes=[pltpu.VMEM((tm, d), F32)],
        compiler_params=_cparams(("parallel", "arbitrary")),
        name="matmul_residual_ln",
    )(a, w, x, g.reshape(1, d), b.reshape(1, d))


def _moe_kernel(te_ref, tv_ref, tok_ref, tok_next_ref, x_hbm, wg_ref, wu_ref, wd_ref, o_ref,
                wg_s, wu_s, wd_s, xbuf, sems):
    i = pl.program_id(0)
    n = pl.num_programs(0)
    slot = lax.rem(i, 2)
    sub = x_hbm.shape[1]
    tm = o_ref.shape[0]

    def rows_of(sl, r):
        return xbuf.at[pl.ds(pl.multiple_of((sl * tm + r) * sub, sub), sub), :]

    def wait_gather(sl):
        whole = xbuf.at[pl.ds(pl.multiple_of(sl * tm * sub, tm * sub), tm * sub), :]
        pltpu.make_async_copy(whole, whole, sems.at[sl]).wait()

    @pl.when(jnp.logical_and(i == 0, tv_ref[0] > 0))
    def _():
        def body(r, carry):
            pltpu.make_async_copy(x_hbm.at[tok_ref[0, r]], rows_of(0, r), sems.at[0]).start()
            return carry
        lax.fori_loop(0, tm, body, 0, unroll=8)

    e = te_ref[i]
    prev = te_ref[jnp.maximum(i - 1, 0)]

    @pl.when(jnp.logical_or(i == 0, e != prev))
    def _():
        wg_s[...] = wg_ref[...].astype(BF16)
        wu_s[...] = wu_ref[...].astype(BF16)
        wd_s[...] = wd_ref[...].astype(BF16)

    @pl.when(tv_ref[i] > 0)
    def _():
        wait_gather(slot)
        for r in range(tm):
            pltpu.make_async_copy(x_hbm.at[tok_next_ref[0, r]], rows_of(1 - slot, r),
                                  sems.at[1 - slot]).start(priority=r % 2)
        base = slot * tm * sub
        x = _unpack_rows([xbuf[pl.ds(base + s, tm, stride=sub), :] for s in range(sub)])
        a = _dot(x, wg_s[...])
        u = _dot(x, wu_s[...])
        o_ref[...] = _dot((_silu(a) * u).astype(BF16), wd_s[...]).astype(o_ref.dtype)

    @pl.when(tv_ref[i] == 0)
    def _():
        o_ref[...] = jnp.zeros_like(o_ref)

    @pl.when(jnp.logical_and(tv_ref[i] == 0, tv_ref[jnp.maximum(i - 1, 0)] > 0))
    def _():
        wait_gather(slot)


def _moe_experts(x_packed, row_tok, tile_expert, tile_valid, w_gate, w_up, w_down, layer):
    sub = PACK_SUB
    t = x_packed.shape[0] // sub
    d = 2 * sub * 128
    n_rows = row_tok.shape[0]
    n_tiles = n_rows // MOE_TM
    de = w_gate.shape[-1]
    toks = row_tok.reshape(n_tiles, 1, MOE_TM)
    tok_spec = lambda off: pl.BlockSpec((None, 1, MOE_TM),
                                        lambda i, te, tv: (jnp.minimum(i + off, n_tiles - 1), 0, 0),
                                        memory_space=pltpu.SMEM)
    grid_spec = pltpu.PrefetchScalarGridSpec(
        num_scalar_prefetch=2,
        grid=(n_tiles,),
        in_specs=[tok_spec(0), tok_spec(1), pl.BlockSpec(memory_space=pl.ANY),
                  pl.BlockSpec((None, None, d, de), lambda i, te, tv: (layer, te[i], 0, 0)),
                  pl.BlockSpec((None, None, d, de), lambda i, te, tv: (layer, te[i], 0, 0)),
                  pl.BlockSpec((None, None, de, d), lambda i, te, tv: (layer, te[i], 0, 0))],
        out_specs=pl.BlockSpec((MOE_TM, d), lambda i, te, tv: (i, 0)),
        scratch_shapes=[pltpu.VMEM((d, de), BF16), pltpu.VMEM((d, de), BF16), pltpu.VMEM((de, d), BF16),
                        pltpu.VMEM((2 * MOE_TM * sub, 128), jnp.uint32), pltpu.SemaphoreType.DMA((2,))],
    )
    return pl.pallas_call(
        _moe_kernel,
        out_shape=_sds((n_rows, d), BF16),
        grid_spec=grid_spec,
        compiler_params=_cparams(("arbitrary",)),
        name="moe_experts",
    )(tile_expert, tile_valid, toks, toks, x_packed.reshape(t, sub, 128), w_gate, w_up, w_down)


def _top_k_desc(vals, k):
    n = vals.shape[-1]
    iota = lax.broadcasted_iota(jnp.int32, vals.shape, vals.ndim - 1)
    top_v, top_i = [], []
    for _ in range(k):
        m = jnp.max(vals, axis=-1, keepdims=True)
        idx = jnp.min(jnp.where(vals == m, iota, n), axis=-1, keepdims=True)
        top_v.append(m)
        top_i.append(idx)
        vals = jnp.where(iota == idx, -jnp.inf, vals)
    return jnp.concatenate(top_v, axis=-1), jnp.concatenate(top_i, axis=-1)


def _route(logits, e_bias):
    t = logits.shape[0]
    scores = jax.nn.sigmoid(logits)
    sel = scores + e_bias.astype(F32)
    per_group = sel.reshape(t, N_GROUPS, N_EXPERTS // N_GROUPS)
    group_score = _top_k_desc(per_group, 2)[0].sum(-1)
    _, g_idx = _top_k_desc(group_score, TOPK_GROUPS)
    g_mask = (g_idx[:, :, None] == jnp.arange(N_GROUPS)[None, None, :]).any(axis=1)
    e_mask = jnp.repeat(g_mask, N_EXPERTS // N_GROUPS, axis=1)
    _, e_idx = _top_k_desc(jnp.where(e_mask, sel, -jnp.inf), TOP_K)
    wts = jnp.take_along_axis(scores, e_idx, axis=1)
    wts = wts / wts.sum(-1, keepdims=True) * ROUTED_SCALE
    return e_idx, wts


def _running_count_kernel(m_ref, o_ref, carry_ref):
    i = pl.program_id(0)

    @pl.when(i == 0)
    def _():
        carry_ref[...] = jnp.zeros_like(carry_ref)

    tm = m_ref.shape[0]
    lower = (lax.broadcasted_iota(jnp.int32, (tm, tm), 0)
             >= lax.broadcasted_iota(jnp.int32, (tm, tm), 1)).astype(BF16)
    counts = _dot(lower, m_ref[...]) + carry_ref[...]
    o_ref[...] = counts
    carry_ref[...] = counts[tm - 1:tm, :]


def _running_count(member, tm):
    t, n = member.shape
    assert t % tm == 0
    return pl.pallas_call(
        _running_count_kernel,
        out_shape=_sds((t, n), F32),
        grid=(t // tm,),
        in_specs=[pl.BlockSpec((tm, n), lambda i: (i, 0))],
        out_specs=pl.BlockSpec((tm, n), lambda i: (i, 0)),
        scratch_shapes=[pltpu.VMEM((1, n), F32)],
        compiler_params=_cparams(("arbitrary",)),
        name="running_count",
    )(member)


def _dispatch_plan(e_idx):
    t = e_idx.shape[0]
    n_assign = t * TOP_K
    member = (e_idx[:, :, None] == jnp.arange(N_EXPERTS)[None, None, :]).any(axis=1)
    running = _running_count(member.astype(BF16), math.gcd(t, 832)).astype(jnp.int32)
    counts = running[-1]
    rank = jnp.take_along_axis(running, e_idx, axis=1) - 1
    padded = (counts + MOE_TM - 1) // MOE_TM * MOE_TM
    pad_end = jnp.cumsum(padded)
    pad_start = pad_end - padded
    dest_of_assign = (pad_start[e_idx] + rank).astype(jnp.int32).reshape(-1)
    n_tiles = (n_assign + N_EXPERTS * (MOE_TM - 1)) // MOE_TM + 1
    tok_of_assign = jnp.repeat(jnp.arange(t, dtype=jnp.int32), TOP_K)
    row_tok = jnp.zeros((n_tiles * MOE_TM,), jnp.int32).at[dest_of_assign].set(tok_of_assign)
    tile_start = jnp.arange(n_tiles) * MOE_TM
    tile_expert = jnp.minimum((tile_start[:, None] >= pad_end[None, :]).sum(axis=1),
                              N_EXPERTS - 1).astype(jnp.int32)
    tile_valid = (tile_start < pad_end[-1]).astype(jnp.int32)
    last_used = tile_expert[jnp.maximum(pad_end[-1] // MOE_TM - 1, 0)]
    tile_expert = jnp.where(tile_valid > 0, tile_expert, last_used)
    return row_tok, tile_expert, tile_valid, dest_of_assign


def _final_kernel(xf_ref, xb_ref, y_ref, wt_ref, p_ref, wsg_ref, wsu_ref, wsd_ref, wpg_ref, wpp_ref,
                  g_ref, b_ref, of_ref, ob_ref):
    wt = wt_ref[...].astype(BF16).astype(F32)
    routed = wt[:, 0:1] * y_ref[0].astype(F32)
    for j in range(1, wt.shape[1]):
        routed = routed + wt[:, j:j + 1] * y_ref[j].astype(F32)
    x = xb_ref[...]
    a = _dot(x, wsg_ref[...])
    u = _dot(x, wsu_ref[...])
    shared = _dot((_silu(a) * u).astype(BF16), wsd_ref[...])
    gate = jax.nn.sigmoid(_dot(x, wpg_ref[...]))
    ple = gate * _dot(p_ref[...], wpp_ref[...])
    v = ALPHA * xf_ref[...] + (routed + shared) + ple
    y = _layer_norm_rows(v, g_ref[...], b_ref[...])
    of_ref[...] = y
    ob_ref[...] = y.astype(BF16)


def _final_block(xf, xb, y_assign, wts, p, wsg, wsu, wsd, wpg, wpp, g, b, tm):
    m, d = xf.shape
    de = wsg.shape[1]
    dp = p.shape[1]
    nk = wts.shape[1]
    assert m % tm == 0
    row = lambda i: (i, 0)
    const = lambda i: (0, 0)
    once = dict(pipeline_mode=pl.Buffered(1))
    return pl.pallas_call(
        _final_kernel,
        out_shape=(_sds((m, d), F32), _sds((m, d), BF16)),
        grid=(m // tm,),
        in_specs=[pl.BlockSpec((tm, d), row), pl.BlockSpec((tm, d), row),
                  pl.BlockSpec((nk, tm, d), lambda i: (0, i, 0)), pl.BlockSpec((tm, nk), row),
                  pl.BlockSpec((tm, dp), row),
                  pl.BlockSpec((d, de), const, **once), pl.BlockSpec((d, de), const, **once),
                  pl.BlockSpec((de, d), const, **once), pl.BlockSpec((d, d), const, **once),
                  pl.BlockSpec((dp, d), const, **once),
                  pl.BlockSpec((1, d), const), pl.BlockSpec((1, d), const)],
        out_specs=(pl.BlockSpec((tm, d), row), pl.BlockSpec((tm, d), row)),
        compiler_params=_cparams(("parallel",)),
        name="shared_ple_ln",
    )(xf, xb, y_assign, wts, p, wsg, wsu, wsd, wpg, wpp, g.reshape(1, d), b.reshape(1, d))


def _attn_prompt_kernel(slopes_ref, q_ref, k_ref, v_ref, o_ref, kk, vv, m_s, l_s, acc_s):
    h = pl.program_id(1)
    sl = pl.program_id(2)

    @pl.when(sl == 0)
    def _():
        kk[0:SLAB, :] = jnp.zeros((SLAB, HEAD_DIM), F32)
        vv[0:SLAB, :] = jnp.zeros((SLAB, HEAD_DIM), F32)

    kk[SLAB:2 * SLAB, :] = k_ref[...]
    vv[SLAB:2 * SLAB, :] = v_ref[...]

    slope = slopes_ref[h]
    ii = lax.broadcasted_iota(jnp.int32, (BAND, BAND), 0)
    jj = lax.broadcasted_iota(jnp.int32, (BAND, BAND), 1)
    dist_prev = (ii + BAND - jj).astype(F32)
    dist_cur = (ii - jj).astype(F32)

    for di, d in enumerate(DILATIONS):
        c = slope * float(d)
        bias_prev = jnp.where(jj >= ii, -c * dist_prev, NEG)
        bias_cur = jnp.where(jj <= ii, -c * dist_cur, NEG)

        def units(it, carry, di=di, d=d, bias_prev=bias_prev, bias_cur=bias_cur):
            def rows(start):
                if d == 1:
                    return pl.ds(start, BAND)
                return pl.ds(start, BAND, stride=d)

            us = [it * ATT_UNROLL + j for j in range(ATT_UNROLL)]
            sp = [u // d for u in us]
            base = [s * (BAND * d) + (u - s * d) for u, s in zip(us, sp)]
            n = range(ATT_UNROLL)
            q = [q_ref[rows(base[j]), :].astype(BF16) for j in n]
            kc = [kk[rows(base[j] + SLAB), :].astype(BF16) for j in n]
            kp = [kk[rows(base[j] + SLAB - BAND * d), :].astype(BF16) for j in n]
            pen = [jnp.where(jnp.logical_and(sl == 0, sp[j] == 0), NEG, 0.0) for j in n]
            s_p = [_dot_nt(q[j], kp[j]) * ATT_SCALE + (bias_prev + pen[j]) for j in n]
            s_c = [_dot_nt(q[j], kc[j]) * ATT_SCALE + bias_cur for j in n]
            m = [jnp.maximum(jnp.max(s_p[j], axis=-1, keepdims=True), jnp.max(s_c[j], axis=-1, keepdims=True))
                 for j in n]
            p_p = [jnp.exp(s_p[j] - m[j]) for j in n]
            p_c = [jnp.exp(s_c[j] - m[j]) for j in n]
            l = [jnp.sum(p_p[j], axis=-1, keepdims=True) + jnp.sum(p_c[j], axis=-1, keepdims=True) for j in n]
            vc = [vv[rows(base[j] + SLAB), :].astype(BF16) for j in n]
            vp = [vv[rows(base[j] + SLAB - BAND * d), :].astype(BF16) for j in n]
            acc = [_dot(p_p[j].astype(BF16), vp[j]) + _dot(p_c[j].astype(BF16), vc[j]) for j in n]
            for j in n:
                m_s[di, rows(base[j]), :] = jnp.broadcast_to(m[j], (BAND, HEAD_DIM))
                l_s[di, rows(base[j]), :] = jnp.broadcast_to(l[j], (BAND, HEAD_DIM))
                acc_s[di, rows(base[j]), :] = acc[j]
            return carry

        lax.fori_loop(0, SLAB // BAND // ATT_UNROLL, units, 0)

    def merge(t, carry):
        rows = pl.ds(pl.multiple_of(t * BAND, BAND), BAND)
        m_max = jnp.maximum(jnp.maximum(m_s[0, rows, :], m_s[1, rows, :]), m_s[2, rows, :])
        w0 = jnp.exp(m_s[0, rows, :] - m_max)
        num = w0 * acc_s[0, rows, :]
        den = w0 * l_s[0, rows, :]
        for di in range(1, len(DILATIONS)):
            w = jnp.exp(m_s[di, rows, :] - m_max)
            num = num + w * acc_s[di, rows, :]
            den = den + w * l_s[di, rows, :]
        o_ref[rows, :] = (num / den).astype(o_ref.dtype)
        return carry

    lax.fori_loop(0, SLAB // BAND, merge, 0)
    kk[0:SLAB, :] = kk[SLAB:2 * SLAB, :]
    vv[0:SLAB, :] = vv[SLAB:2 * SLAB, :]


def _attn_prompt(proj, slopes, n_batch, seq):
    assert seq % SLAB == 0
    ns = seq // SLAB
    grid_spec = pltpu.PrefetchScalarGridSpec(
        num_scalar_prefetch=1,
        grid=(n_batch, ATT_HEADS, ns),
        in_specs=[pl.BlockSpec((SLAB, HEAD_DIM), lambda b, h, s, sl: (b * ns + s, h)),
                  pl.BlockSpec((SLAB, HEAD_DIM), lambda b, h, s, sl: (b * ns + s, ATT_HEADS + h)),
                  pl.BlockSpec((SLAB, HEAD_DIM), lambda b, h, s, sl: (b * ns + s, 2 * ATT_HEADS + h))],
        out_specs=pl.BlockSpec((SLAB, HEAD_DIM), lambda b, h, s, sl: (b * ns + s, h)),
        scratch_shapes=[pltpu.VMEM((2 * SLAB, HEAD_DIM), F32), pltpu.VMEM((2 * SLAB, HEAD_DIM), F32),
                        pltpu.VMEM((len(DILATIONS), SLAB, HEAD_DIM), F32),
                        pltpu.VMEM((len(DILATIONS), SLAB, HEAD_DIM), F32),
                        pltpu.VMEM((len(DILATIONS), SLAB, HEAD_DIM), F32)],
    )
    return pl.pallas_call(
        _attn_prompt_kernel,
        out_shape=_sds((n_batch * seq, ATT_W), BF16),
        grid_spec=grid_spec,
        compiler_params=_cparams(("parallel", "parallel", "arbitrary")),
        name="attn_prompt",
    )(slopes, proj, proj, proj)


def _attn_sample_kernel(slopes_ref, q_ref, kn_ref, vn_ref, k_ref, v_ref, o_ref):
    hb = pl.program_id(1)
    n_heads, n_buf = k_ref.shape[0], k_ref.shape[1]
    steps = (BAND - lax.broadcasted_iota(jnp.int32, (BAND, 1), 0)).astype(F32)
    for hh in range(n_heads):
        h = hb * n_heads + hh
        head = pl.ds(h, 1)
        q = q_ref[head, :]
        v_self = vn_ref[head, :]
        slope = slopes_ref[h]
        s_self = jnp.sum(q * kn_ref[head, :], axis=-1, keepdims=True) * ATT_SCALE
        parts = []
        for d in DILATIONS:
            start = n_buf - BAND * d
            rows = pl.ds(start, BAND) if d == 1 else pl.ds(start, BAND, stride=d)
            sc = (jnp.sum(k_ref[hh, rows, :] * q, axis=-1, keepdims=True) * ATT_SCALE
                  - slope * (steps * float(d)))
            m_d = jnp.maximum(jnp.max(sc, axis=0, keepdims=True), s_self)
            p = jnp.exp(sc - m_d)
            p_self = jnp.exp(s_self - m_d)
            l_d = jnp.sum(p, axis=0, keepdims=True) + p_self
            acc = jnp.sum(p * v_ref[hh, rows, :], axis=0, keepdims=True) + p_self * v_self
            parts.append((m_d, l_d, acc))
        m_max = jnp.maximum(jnp.maximum(parts[0][0], parts[1][0]), parts[2][0])
        num = jnp.zeros(q.shape, F32)
        den = jnp.zeros(s_self.shape, F32)
        for m_d, l_d, acc in parts:
            w = jnp.exp(m_d - m_max)
            num = num + w * acc
            den = den + w * l_d
        o_ref[head, :] = num / den


def _attn_sample(q, k_new, v_new, k_buf, v_buf, slopes):
    nb, nh, n_buf, e = k_buf.shape
    hpb = math.gcd(nh, ATT_SAMPLE_HEADS)
    assert n_buf >= BAND * DILATIONS[-1]
    tok = pl.BlockSpec((None, nh, e), lambda b, h, sl: (b, 0, 0))
    buf = pl.BlockSpec((None, hpb, n_buf, e), lambda b, h, sl: (b, h, 0, 0))
    grid_spec = pltpu.PrefetchScalarGridSpec(
        num_scalar_prefetch=1, grid=(nb, nh // hpb), in_specs=[tok, tok, tok, buf, buf], out_specs=tok)
    return pl.pallas_call(
        _attn_sample_kernel,
        out_shape=_sds((nb, nh, e), F32),
        grid_spec=grid_spec,
        compiler_params=_cparams(("parallel", "arbitrary")),
        name="attn_sample",
    )(slopes, q, k_new, v_new, k_buf, v_buf)


def _shifted(u, prev8, shift):
    row = lax.broadcasted_iota(jnp.int32, u.shape, 0)
    out = pltpu.roll(u, shift, 0)
    for s in range(shift):
        out = jnp.where(row == s, prev8[8 - shift + s:8 - shift + s + 1, :], out)
    return out


def _sconv_prompt_kernel(h_ref, gb_ref, gc_ref, hp_ref, gcp_ref, w_ref, o_ref):
    i = pl.program_id(1)
    u = gc_ref[...] * h_ref[...]
    up = jnp.where(i == 0, 0.0, gcp_ref[...] * hp_ref[...])
    w = w_ref[...]
    conv = w[0:1] * _shifted(u, up, 2) + w[1:2] * _shifted(u, up, 1) + w[2:3] * u
    o_ref[...] = (gb_ref[...] * conv).astype(o_ref.dtype)


def _sconv_prompt(proj, w, n_batch, seq, tm):
    sc = w.shape[1]
    nt = seq // tm
    cb = 3 * ATT_W // sc
    cur = lambda c: pl.BlockSpec((tm, sc), lambda b, i: (b * nt + i, cb + c))
    prev = lambda c: pl.BlockSpec((8, sc), lambda b, i: (jnp.maximum((b * nt + i) * (tm // 8) - 1, 0), cb + c))
    return pl.pallas_call(
        _sconv_prompt_kernel,
        out_shape=_sds((n_batch * seq, sc), BF16),
        grid=(n_batch, nt),
        in_specs=[cur(0), cur(1), cur(2), prev(0), prev(2), pl.BlockSpec((SC_WIDTH, sc), lambda b, i: (0, 0))],
        out_specs=pl.BlockSpec((tm, sc), lambda b, i: (b * nt + i, 0)),
        compiler_params=_cparams(("parallel", "parallel")),
        name="sconv_prompt",
    )(proj, proj, proj, proj, proj, w)


def _sconv_sample_kernel(h_ref, gb_ref, gc_ref, p0_ref, p1_ref, w_ref, o_ref, u_ref):
    u = gc_ref[...] * h_ref[...]
    w = w_ref[...]
    conv = w[0:1] * p0_ref[...] + w[1:2] * p1_ref[...] + w[2:3] * u
    o_ref[...] = (gb_ref[...] * conv).astype(o_ref.dtype)
    u_ref[...] = u


def _sconv_sample(h, gb, gc, p0, p1, w):
    nb, sc = h.shape
    return pl.pallas_call(
        _sconv_sample_kernel,
        out_shape=(_sds((nb, sc), BF16), _sds((nb, sc), F32)),
        name="sconv_sample",
    )(h, gb, gc, p0, p1, w)


def _qk_normalise(y, cb, n_qk_blocks):
    scale = jnp.where(cb < n_qk_blocks // 2, DN_DIM ** -0.5, 1.0)
    segs = []
    for s in range(y.shape[1] // DN_DIM):
        seg = y[:, s * DN_DIM:(s + 1) * DN_DIM]
        ss = jnp.sum(seg * seg, axis=-1, keepdims=True)
        segs.append(seg * (lax.rsqrt(ss + L2_EPS) * scale))
    return jnp.concatenate(segs, axis=1)


def _dconv_prompt_kernel(x_ref, xp_ref, w_ref, o_ref, *, n_qk_blocks):
    cb = pl.program_id(0)
    i = pl.program_id(2)
    x = x_ref[...]
    xp = jnp.where(i == 0, 0.0, xp_ref[...])
    w = w_ref[...]
    conv = (w[0:1] * _shifted(x, xp, 3) + w[1:2] * _shifted(x, xp, 2)
            + w[2:3] * _shifted(x, xp, 1) + w[3:4] * x)
    y = _silu(conv)

    @pl.when(cb < n_qk_blocks)
    def _():
        o_ref[...] = _qk_normalise(y, cb, n_qk_blocks)

    @pl.when(cb >= n_qk_blocks)
    def _():
        o_ref[...] = y


def _dconv_prompt(proj, w, n_batch, seq, tm, tc):
    nt = seq // tm
    ncb = DN_CONV_DIM // tc
    return pl.pallas_call(
        functools.partial(_dconv_prompt_kernel, n_qk_blocks=2 * DN_QK_W // tc),
        out_shape=_sds((n_batch * seq, DN_CONV_DIM), F32),
        grid=(ncb, n_batch, nt),
        in_specs=[pl.BlockSpec((tm, tc), lambda c, b, i: (b * nt + i, c)),
                  pl.BlockSpec((8, tc), lambda c, b, i: (jnp.maximum((b * nt + i) * (tm // 8) - 1, 0), c)),
                  pl.BlockSpec((DN_CONV_WIDTH, tc), lambda c, b, i: (0, c))],
        out_specs=pl.BlockSpec((tm, tc), lambda c, b, i: (b * nt + i, c)),
        compiler_params=_cparams(("parallel", "parallel", "parallel")),
        name="dconv_prompt",
    )(proj, proj, w)


def _dconv_sample_kernel(x_ref, p0_ref, p1_ref, p2_ref, w_ref, o_ref, *, n_qk_blocks):
    cb = pl.program_id(0)
    w = w_ref[...]
    conv = w[0:1] * p0_ref[...] + w[1:2] * p1_ref[...] + w[2:3] * p2_ref[...] + w[3:4] * x_ref[...]
    y = _silu(conv)

    @pl.when(cb < n_qk_blocks)
    def _():
        o_ref[...] = _qk_normalise(y, cb, n_qk_blocks)

    @pl.when(cb >= n_qk_blocks)
    def _():
        o_ref[...] = y


def _dconv_sample(x, p0, p1, p2, w, tc):
    nb = x.shape[0]
    blk = pl.BlockSpec((nb, tc), lambda c: (0, c))
    return pl.pallas_call(
        functools.partial(_dconv_sample_kernel, n_qk_blocks=2 * DN_QK_W // tc),
        out_shape=_sds((nb, DN_CONV_DIM), F32),
        grid=(DN_CONV_DIM // tc,),
        in_specs=[blk, blk, blk, blk, pl.BlockSpec((DN_CONV_WIDTH, tc), lambda c: (0, c))],
        out_specs=blk,
        compiler_params=_cparams(("parallel",)),
        name="dconv_sample",
    )(x, p0, p1, p2, w)


def _gated_rms(o, z, norm_w):
    return o * lax.rsqrt(jnp.mean(o * o, axis=-1, keepdims=True) + RMS_EPS) * norm_w * _silu(z)


def _delta_prompt_kernel(q_ref, k_ref, v_ref, z_ref, g_ref, gt_ref, beta_ref, nw_ref, o_ref, s_out_ref, s_ref,
                         *, n_chunks):
    c = pl.program_id(2)

    @pl.when(c == 0)
    def _():
        s_ref[...] = jnp.zeros_like(s_ref)

    cc = DN_CHUNK
    heads = range(DN_GROUP_V)
    ii = lax.broadcasted_iota(jnp.int32, (cc, cc), 0)
    jj = lax.broadcasted_iota(jnp.int32, (cc, cc), 1)
    beta_all = beta_ref[...]
    norm_w = nw_ref[...]

    lower = (ii >= jj).astype(BF16)
    gc_all = _dot_exact_lhs(lower, g_ref[...])
    kk = lax.broadcasted_iota(jnp.int32, (2 * cc, cc), 0) - lax.rem(c, 2) * cc
    tt = lax.broadcasted_iota(jnp.int32, (2 * cc, cc), 1)
    upto = jnp.logical_and(kk >= 0, kk <= tt).astype(BF16)
    gr_all = _dot_exact_rhs(gt_ref[...], upto)

    q = [q_ref[:, h * DN_DIM:(h + 1) * DN_DIM] for h in range(DN_GROUP_QK)]
    k = [k_ref[:, h * DN_DIM:(h + 1) * DN_DIM] for h in range(DN_GROUP_QK)]
    k_b = [x.astype(BF16) for x in k]
    qk = [_dot_nt(q[h].astype(BF16), k_b[h]) for h in range(DN_GROUP_QK)]

    gc = [gc_all[:, h:h + 1] for h in heads]
    beta = [beta_all[:, h:h + 1] for h in heads]
    decay = [jnp.where(ii >= jj, jnp.exp(jnp.minimum(gc[h] - gr_all[h:h + 1, :], 0.0)), 0.0) for h in heads]
    eg = [jnp.exp(gc[h]) for h in heads]
    kbeta = [k[h // 2] * beta[h] for h in heads]
    npow = [-jnp.where(ii > jj, _dot_nt(kbeta[h].astype(BF16), k_b[h // 2]) * decay[h], 0.0) for h in heads]
    y = [jnp.concatenate([v_ref[:, h * DN_DIM:(h + 1) * DN_DIM] * beta[h], kbeta[h] * eg[h]], axis=1)
         for h in heads]
    for step in range(6):
        upd = [_dot3(npow[h], y[h]) for h in heads]
        y = [y[h] + upd[h] for h in heads]
        if step < 5:
            npow = [_dot3(npow[h], npow[h]) for h in heads]
    s_old = [s_ref[h] for h in heads]
    s_b = [x.astype(BF16) for x in s_old]
    v_new = [y[h][:, :DN_DIM] - _dot(y[h][:, DN_DIM:].astype(BF16), s_b[h]) for h in heads]
    v_new_b = [x.astype(BF16) for x in v_new]
    o = [_dot((q[h // 2] * eg[h]).astype(BF16), s_b[h]) + _dot((qk[h // 2] * decay[h]).astype(BF16), v_new_b[h])
         for h in heads]
    for h in heads:
        g_last = gc[h][cc - 1:cc, :]
        k_dec = k[h // 2] * jnp.exp(g_last - gc[h])
        s_ref[h] = s_old[h] * jnp.exp(g_last) + _dot_tn(k_dec.astype(BF16), v_new_b[h])
    for h in heads:
        z = z_ref[:, h * DN_DIM:(h + 1) * DN_DIM]
        o_ref[:, h * DN_DIM:(h + 1) * DN_DIM] = _gated_rms(o[h], z, norm_w).astype(o_ref.dtype)

    @pl.when(c == n_chunks - 1)
    def _():
        s_out_ref[...] = s_ref[...]


def _delta_prompt(qkv, proj, g, g_t, beta, norm_w, n_batch, seq):
    assert seq % (2 * DN_CHUNK) == 0
    nc = seq // DN_CHUNK
    n_hg = DN_QK_HEADS // DN_GROUP_QK
    wq = DN_GROUP_QK * DN_DIM
    wv = DN_GROUP_V * DN_DIM
    row = lambda b, hg, c: b * nc + c
    return pl.pallas_call(
        functools.partial(_delta_prompt_kernel, n_chunks=nc),
        out_shape=(_sds((n_batch * seq, DN_VW), BF16), _sds((n_batch, DN_V_HEADS, DN_DIM, DN_DIM), F32)),
        grid=(n_batch, n_hg, nc),
        in_specs=[pl.BlockSpec((DN_CHUNK, wq), lambda b, hg, c: (row(b, hg, c), hg)),
                  pl.BlockSpec((DN_CHUNK, wq), lambda b, hg, c: (row(b, hg, c), DN_QK_W // wq + hg)),
                  pl.BlockSpec((DN_CHUNK, wv), lambda b, hg, c: (row(b, hg, c), 2 * DN_QK_W // wv + hg)),
                  pl.BlockSpec((DN_CHUNK, wv), lambda b, hg, c: (row(b, hg, c), DN_CONV_DIM // wv + hg)),
                  pl.BlockSpec((None, DN_CHUNK, DN_GROUP_V), lambda b, hg, c: (hg, row(b, hg, c), 0)),
                  pl.BlockSpec((None, DN_GROUP_V, 2 * DN_CHUNK), lambda b, hg, c: (hg, 0, row(b, hg, c) // 2)),
                  pl.BlockSpec((None, DN_CHUNK, DN_GROUP_V), lambda b, hg, c: (hg, row(b, hg, c), 0)),
                  pl.BlockSpec((1, DN_DIM), lambda b, hg, c: (0, 0))],
        out_specs=(pl.BlockSpec((DN_CHUNK, wv), lambda b, hg, c: (row(b, hg, c), hg)),
                   pl.BlockSpec((None, DN_GROUP_V, DN_DIM, DN_DIM), lambda b, hg, c: (b, hg, 0, 0))),
        scratch_shapes=[pltpu.VMEM((DN_GROUP_V, DN_DIM, DN_DIM), F32)],
        compiler_params=_cparams(("parallel", "parallel", "arbitrary")),
        name="delta_prompt",
    )(qkv, qkv, qkv, proj, g, g_t, beta, norm_w.reshape(1, DN_DIM))


def _delta_sample_kernel(qt_ref, kt_ref, v_ref, z_ref, g_ref, beta_ref, nw_ref, s_ref, o_ref, s_out_ref):
    norm_w = nw_ref[...]
    qt = qt_ref[...]
    kt = kt_ref[...]
    for hv in range(DN_V_HEADS):
        hq = hv // 2
        k_col = jnp.broadcast_to(kt[:, hq:hq + 1], (DN_DIM, DN_DIM))
        q_col = jnp.broadcast_to(qt[:, hq:hq + 1], (DN_DIM, DN_DIM))
        eg = jnp.exp(g_ref[hv:hv + 1, :])
        beta = beta_ref[hv:hv + 1, :]
        s = s_ref[hv]
        sk = jnp.sum(s * k_col, axis=0, keepdims=True)
        delta = beta * (v_ref[hv:hv + 1, :] - eg * sk)
        s_new = s * eg + k_col * delta
        s_out_ref[hv] = s_new
        o = jnp.sum(s_new * q_col, axis=0, keepdims=True)
        o_ref[hv:hv + 1, :] = _gated_rms(o, z_ref[hv:hv + 1, :], norm_w).astype(o_ref.dtype)


def _delta_sample(qt, kt, v, z, g_b, beta_b, norm_w, s0):
    nb = v.shape[0]
    per_b3 = lambda shape: pl.BlockSpec((None,) + shape, lambda b: (b, 0, 0))
    st = pl.BlockSpec((None, DN_V_HEADS, DN_DIM, DN_DIM), lambda b: (b, 0, 0, 0))
    return pl.pallas_call(
        _delta_sample_kernel,
        out_shape=(_sds((nb, DN_V_HEADS, DN_DIM), BF16), _sds(s0.shape, F32)),
        grid=(nb,),
        in_specs=[per_b3((DN_DIM, DN_QK_HEADS)), per_b3((DN_DIM, DN_QK_HEADS)),
                  per_b3((DN_V_HEADS, DN_DIM)), per_b3((DN_V_HEADS, DN_DIM)),
                  per_b3((DN_V_HEADS, DN_DIM)), per_b3((DN_V_HEADS, DN_DIM)),
                  pl.BlockSpec((1, DN_DIM), lambda b: (0, 0)), st],
        out_specs=(per_b3((DN_V_HEADS, DN_DIM)), st),
        compiler_params=_cparams(("parallel",)),
        name="delta_sample",
    )(qt, kt, v, z, g_b, beta_b, norm_w.reshape(1, DN_DIM), s0)


def _mixer_layer(xf, xb, w_in, sconv_w, w_out, k_buf, v_buf, sconv_prev, n_batch, seq, g1, b1):
    tp = n_batch * seq
    proj = _matmul(xb, w_in.astype(BF16), 832, 512)
    slopes = jnp.exp2(-8.0 * (jnp.arange(ATT_HEADS, dtype=F32) + 1.0) / ATT_HEADS)
    att_p = _attn_prompt(proj, slopes, n_batch, seq)
    ns = xf.shape[0] - tp
    ps = proj[tp:]
    q_s, k_s, v_s = (ps[:, i * ATT_W:(i + 1) * ATT_W] for i in range(3))
    heads = lambda a: a.reshape(ns, ATT_HEADS, HEAD_DIM)
    att_s = _attn_sample(heads(q_s), heads(k_s), heads(v_s), k_buf.transpose(0, 2, 1, 3),
                         v_buf.transpose(0, 2, 1, 3), slopes).astype(BF16)
    sc = sconv_w.shape[1]
    gc_p = _sconv_prompt(proj, sconv_w, n_batch, seq, 512)
    h_s, gb_s, gcs_s = (ps[:, 3 * ATT_W + i * sc:3 * ATT_W + (i + 1) * sc] for i in range(3))
    gc_s, u_s = _sconv_sample(h_s, gb_s, gcs_s, sconv_prev[:, 0], sconv_prev[:, 1], sconv_w)
    mixed = jnp.concatenate([jnp.concatenate([att_p, att_s.reshape(ns, ATT_W)], axis=0),
                             jnp.concatenate([gc_p, gc_s], axis=0)], axis=1)
    x1 = _matmul_residual_ln(mixed, w_out.astype(BF16), xf, g1, b1, 320, w_out.shape[0])

    keep = min(SLAB, seq)
    tail = lambda n, c0, c1: jnp.stack([proj[(b + 1) * seq - n:(b + 1) * seq, c0:c1] for b in range(n_batch)])
    k_rows_p = tail(keep, ATT_W, 2 * ATT_W).reshape(n_batch, keep, ATT_HEADS, HEAD_DIM)
    v_rows_p = tail(keep, 2 * ATT_W, 3 * ATT_W).reshape(n_batch, keep, ATT_HEADS, HEAD_DIM)
    u_tail = (tail(SC_WIDTH - 1, 3 * ATT_W + 2 * sc, 3 * ATT_W + 3 * sc)
              * tail(SC_WIDTH - 1, 3 * ATT_W, 3 * ATT_W + sc))
    sconv_new_s = jnp.stack([sconv_prev[:, 1], u_s], axis=1)
    outs_p = (k_rows_p, v_rows_p, u_tail)
    outs_s = (k_s.reshape(ns, 1, ATT_HEADS, HEAD_DIM), v_s.reshape(ns, 1, ATT_HEADS, HEAD_DIM), sconv_new_s)
    return x1, outs_p, outs_s


def _delta_layer(xf, xb, w_in, conv_w, a_log, dt_bias, norm_w, w_out, conv_prev, s0, n_batch, seq, g1, b1):
    tp = n_batch * seq
    ns = xf.shape[0] - tp
    n_main = DN_CONV_DIM + DN_VW
    proj = _matmul(xb, w_in[:, :n_main].astype(BF16), 832, 512)
    ba = _matmul(xb, w_in[:, n_main:].astype(BF16), 832, 2 * DN_V_HEADS)
    beta = jax.nn.sigmoid(ba[:, :DN_V_HEADS])
    g = -jnp.exp(a_log.astype(F32)) * jax.nn.softplus(ba[:, DN_V_HEADS:] + dt_bias.astype(F32))

    qkv_p = _dconv_prompt(proj, conv_w, n_batch, seq, 512, 512)
    n_vg = DN_V_HEADS // DN_GROUP_V
    grp = lambda a: a[:tp].reshape(tp, n_vg, DN_GROUP_V).transpose(1, 0, 2)
    o_p, s_p = _delta_prompt(qkv_p, proj, grp(g), grp(g).transpose(0, 2, 1), grp(beta), norm_w, n_batch, seq)

    ps = proj[tp:]
    qkv_s = _dconv_sample(ps[:, :DN_CONV_DIM], conv_prev[:, 0], conv_prev[:, 1], conv_prev[:, 2], conv_w, 512)
    qt = qkv_s[:, :DN_QK_W].reshape(ns, DN_QK_HEADS, DN_DIM).transpose(0, 2, 1)
    kt = qkv_s[:, DN_QK_W:2 * DN_QK_W].reshape(ns, DN_QK_HEADS, DN_DIM).transpose(0, 2, 1)
    v_s = qkv_s[:, 2 * DN_QK_W:].reshape(ns, DN_V_HEADS, DN_DIM)
    z_s = ps[:, DN_CONV_DIM:].reshape(ns, DN_V_HEADS, DN_DIM)
    lanes = lambda a: jnp.broadcast_to(a[tp:, :, None], (ns, DN_V_HEADS, DN_DIM))
    o_s, s_s = _delta_sample(qt, kt, v_s, z_s, lanes(g), lanes(beta), norm_w, s0)

    og = jnp.concatenate([o_p, o_s.reshape(ns, DN_VW)], axis=0)
    x1 = _matmul_residual_ln(og, w_out.astype(BF16), xf, g1, b1, 320, 2048)

    dconv_new_p = jnp.stack([proj[(b + 1) * seq - (DN_CONV_WIDTH - 1):(b + 1) * seq, :DN_CONV_DIM]
                             for b in range(n_batch)])
    dconv_new_s = jnp.concatenate([conv_prev[:, 1:], ps[:, None, :DN_CONV_DIM]], axis=1)
    return x1, (dconv_new_p, s_p), (dconv_new_s, s_s)


def _moe_block(x1, p_b, w_router, e_bias, w_gate, w_up, w_down, layer, ws_gate, ws_up, ws_down,
               ple_w_proj, ple_w_gate, g2, b2):
    x1f, x1b, x1p = x1
    t, d = x1f.shape
    logits = _matmul(x1b, w_router.astype(BF16), 832, N_EXPERTS)
    e_idx, wts = _route(logits, e_bias)
    row_tok, tile_expert, tile_valid, dest_of_assign = _dispatch_plan(e_idx)
    y_rows = _moe_experts(x1p, row_tok, tile_expert, tile_valid, w_gate, w_up, w_down, layer)
    y_assign = y_rows[dest_of_assign.reshape(t, TOP_K).T.reshape(-1)].reshape(TOP_K, t, d)
    return _final_block(x1f, x1b, y_assign, wts, p_b, ws_gate.astype(BF16), ws_up.astype(BF16),
                        ws_down.astype(BF16), ple_w_gate.astype(BF16), ple_w_proj.astype(BF16), g2, b2, 160)


def kernel(x_prompt, x_sample, cache_attn_k, cache_attn_v, state_sconv, state_dconv, state_delta,
           p_prompt, p_sample, mix_w_in, mix_sconv_w, mix_w_out, dn_w_in, dn_conv_w, dn_a_log,
           dn_dt_bias, dn_norm_w, dn_w_out, ln1_g, ln1_b, ln2_g, ln2_b, moe_w_router, moe_e_bias,
           moe_w_gate, moe_w_up, moe_w_down, moe_ws_gate, moe_ws_up, moe_ws_down, ple_w_proj,
           ple_w_gate):
    n_batch, seq, d = x_prompt.shape
    ns = x_sample.shape[0]
    assert x_sample.shape[1] == 1
    tp = n_batch * seq
    xf = jnp.concatenate([x_prompt.reshape(tp, d), x_sample.reshape(ns, d)], axis=0)
    xb = xf.astype(BF16)
    p_all = jnp.concatenate([p_prompt.reshape(DEPTH, tp, -1), p_sample.reshape(DEPTH, ns, -1)],
                            axis=1).astype(BF16)

    mix_p, mix_s, dn_p, dn_s = [], [], [], []
    for i in range(DEPTH):
        j = i // 2
        if i % 2 == 0:
            x1, o_p, o_s = _mixer_layer(
                xf, xb, mix_w_in[j], mix_sconv_w[j], mix_w_out[j], cache_attn_k[j], cache_attn_v[j],
                state_sconv[j], n_batch, seq, ln1_g[i], ln1_b[i])
            mix_p.append(o_p)
            mix_s.append(o_s)
        else:
            x1, o_p, o_s = _delta_layer(
                xf, xb, dn_w_in[j], dn_conv_w[j], dn_a_log[j], dn_dt_bias[j], dn_norm_w[j], dn_w_out[j],
                state_dconv[j], state_delta[j], n_batch, seq, ln1_g[i], ln1_b[i])
            dn_p.append(o_p)
            dn_s.append(o_s)
        xf, xb = _moe_block(x1, p_all[i], moe_w_router[i], moe_e_bias[i], moe_w_gate,
                            moe_w_up, moe_w_down, i, moe_ws_gate[i], moe_ws_up[i], moe_ws_down[i],
                            ple_w_proj[i], ple_w_gate[i], ln2_g[i], ln2_b[i])

    stack = lambda parts, idx: jnp.stack([p[idx] for p in parts])
    y_prompt = xf[:tp].reshape(n_batch, seq, d)
    y_sample = xf[tp:].reshape(ns, 1, d)
    return (y_prompt, y_sample,
            stack(mix_p, 0), stack(mix_p, 1), stack(mix_p, 2), stack(dn_p, 0), stack(dn_p, 1),
            stack(mix_s, 0), stack(mix_s, 1), stack(mix_s, 2), stack(dn_s, 0), stack(dn_s, 1))
```
